```python
import math
import jax, jax.numpy as jnp
from jax import lax
import numpy as np

D_MODEL = 1024
BATCH = 16
SEQ = 2048
DEPTH = 4

N_META = 16
D_INNER = 2 * D_MODEL
D_ATTN = D_INNER // 4
D_CONV = D_INNER // 4
D_SSM = D_INNER // 2
ATTN_HEADS = 4
ATTN_QK_DIM = D_ATTN // (2 * ATTN_HEADS)
ATTN_V_DIM = 2 * ATTN_QK_DIM
SCONV_WIDTH = 3
SSM_HEAD_DIM = 64
SSM_HEADS = D_SSM // SSM_HEAD_DIM
SSM_GROUPS = 4
SSM_STATE = 128
SSM_CONV_WIDTH = 4
SSM_CONV_DIM = D_SSM + 2 * SSM_GROUPS * SSM_STATE
CHUNK = 128
N_PAD = CHUNK - N_META
NORM_EPS = 1e-5
MASK_VALUE = -1e30
IN_SPLIT_SIZES = (D_ATTN, D_ATTN, D_ATTN, D_ATTN,
                  D_CONV, D_CONV, D_CONV, D_CONV,
                  D_SSM, SSM_CONV_DIM, SSM_HEADS)
N_IN = sum(IN_SPLIT_SIZES)
IN_SPLIT_POINTS = tuple(int(s) for s in np.cumsum(IN_SPLIT_SIZES)[:-1])

kernel_name = "hybrid_diffattn_shortconv_ssd_meta"


def rmsnorm(x, w):
    xf = x.astype(jnp.float32)
    y = xf * lax.rsqrt(jnp.mean(xf * xf, axis=-1, keepdims=True) + NORM_EPS)
    return (y * w.astype(jnp.float32)).astype(x.dtype)


def gated_group_rmsnorm(y, z, w):
    u = (y.astype(jnp.float32) * jax.nn.silu(z.astype(jnp.float32)))
    shp = u.shape
    u = u.reshape(shp[:-1] + (SSM_GROUPS, shp[-1] // SSM_GROUPS))
    u = u * lax.rsqrt(jnp.mean(u * u, axis=-1, keepdims=True) + NORM_EPS)
    return (u.reshape(shp) * w.astype(jnp.float32)).astype(z.dtype)


def causal_dwconv(u, w):
    k, c = w.shape
    return lax.conv_general_dilated(
        u, w[:, None, :].astype(u.dtype), window_strides=(1,), padding=[(k - 1, 0)],
        dimension_numbers=('NWC', 'WIO', 'NWC'), feature_group_count=c)


def pad_front(a):
    return jnp.pad(a, [(0, 0), (N_PAD, 0)] + [(0, 0)] * (a.ndim - 2))


def diff_attention(q, k, v, lam):
    b, lp = q.shape[:2]
    nblk = lp // CHUNK
    scale = ATTN_QK_DIM ** -0.5
    kpos = jnp.arange(lp)
    qb = q.reshape(b, nblk, CHUNK, ATTN_HEADS, 2, ATTN_QK_DIM).swapaxes(0, 1)
    starts = jnp.arange(nblk) * CHUNK

    def one_block(args):
        qi, s0 = args
        qpos = s0 + jnp.arange(CHUNK)
        mask = (kpos[None, :] <= qpos[:, None]) & (kpos[None, :] >= N_PAD)
        s = jnp.einsum('bqhmd,bkhmd->bhmqk', qi, k).astype(jnp.float32) * scale
        p = jax.nn.softmax(jnp.where(mask, s, MASK_VALUE), axis=-1)
        a = p[:, :, 0] - lam * p[:, :, 1]
        return jnp.einsum('bhqk,bkhe->bqhe', a.astype(v.dtype), v)

    o = lax.map(one_block, (qb, starts))
    return o.swapaxes(0, 1).reshape(b, lp, ATTN_HEADS, ATTN_V_DIM)


def diff_attention_branch(q, k, v, lam_params, sub_norm_w, layer):
    b, l = q.shape[:2]
    lam_init = 0.8 - 0.6 * math.exp(-0.3 * layer)
    lp = lam_params.astype(jnp.float32)
    lam = jnp.exp(jnp.sum(lp[0] * lp[1])) - jnp.exp(jnp.sum(lp[2] * lp[3])) + lam_init
    q = pad_front(q.reshape(b, l, ATTN_HEADS, 2, ATTN_QK_DIM))
    k = pad_front(k.reshape(b, l, ATTN_HEADS, 2, ATTN_QK_DIM))
    v = pad_front(v.reshape(b, l, ATTN_HEADS, ATTN_V_DIM))
    o = diff_attention(q, k, v, lam)[:, N_PAD:]
    o = rmsnorm(o, sub_norm_w) * (1.0 - lam_init)
    return o.reshape(b, l, D_ATTN)


def short_conv_branch(gate_b, gate_c, hin, conv_w):
    return gate_b * causal_dwconv(gate_c * hin, conv_w)


def ssd_scan(x, dt, a, bm, cm):
    b, lp, nh, p = x.shape
    g, n = bm.shape[-2:]
    r = nh // g
    c = lp // CHUNK
    xc = (x.astype(jnp.float32) * dt[..., None]).reshape(b, c, CHUNK, g, r, p)
    bc = bm.astype(jnp.float32).reshape(b, c, CHUNK, g, n)
    cc = cm.astype(jnp.float32).reshape(b, c, CHUNK, g, n)
    adt = jnp.transpose((a * dt).reshape(b, c, CHUNK, g, r), (0, 1, 3, 4, 2))
    acum = jnp.cumsum(adt, axis=-1)
    causal = jnp.tril(jnp.ones((CHUNK, CHUNK), dtype=bool))
    seg = acum[..., :, None] - acum[..., None, :]
    lmat = jnp.exp(jnp.where(causal, seg, -jnp.inf))
    cb = jnp.einsum('bclgn,bcsgn->bcgls', cc, bc)
    y_diag = jnp.einsum('bcgls,bcgrls,bcsgrp->bclgrp', cb, lmat, xc)
    decay_s = jnp.exp(acum[..., -1:] - acum)
    states = jnp.einsum('bclgn,bcgrl,bclgrp->bcgrpn', bc, decay_s, xc)
    chunk_decay = jnp.exp(acum[..., -1])

    def step(carry, inp):
        st, dec = inp
        return carry * dec[..., None, None] + st, carry

    init = jnp.zeros((b, g, r, p, n), jnp.float32)
    _, prev = lax.scan(step, init, (states.swapaxes(0, 1), chunk_decay.swapaxes(0, 1)))
    prev = prev.swapaxes(0, 1)
    y_off = jnp.einsum('bclgn,bcgrpn,bcgrl->bclgrp', cc, prev, jnp.exp(acum))
    return (y_diag + y_off).reshape(b, lp, nh, p)


def ssd_branch(z, xbc, dt_raw, conv_w, conv_b, a_log, dt_bias, d_skip, norm_w):
    b, l = z.shape[:2]
    xbc = jax.nn.silu(causal_dwconv(xbc, conv_w) + conv_b.astype(xbc.dtype))
    xs, bm, cm = jnp.split(xbc, [D_SSM, D_SSM + SSM_GROUPS * SSM_STATE], axis=-1)
    xs = xs.reshape(b, l, SSM_HEADS, SSM_HEAD_DIM)
    bm = bm.reshape(b, l, SSM_GROUPS, SSM_STATE)
    cm = cm.reshape(b, l, SSM_GROUPS, SSM_STATE)
    dt = jax.nn.softplus(dt_raw.astype(jnp.float32) + dt_bias.astype(jnp.float32))
    a = -jnp.exp(a_log.astype(jnp.float32))
    y = ssd_scan(pad_front(xs), pad_front(dt), a, pad_front(bm), pad_front(cm))[:, N_PAD:]
    y = y + xs.astype(jnp.float32) * d_skip.astype(jnp.float32)[:, None]
    return gated_group_rmsnorm(y.reshape(b, l, D_SSM), z, norm_w)


def setup_inputs(seed: int = 0) -> dict:
    key = jax.random.key(seed)
    ks = jax.random.split(key, 16)
    nrm = jax.random.normal
    dt0 = jnp.exp(jax.random.uniform(ks[11], (DEPTH, SSM_HEADS), minval=math.log(1e-3), maxval=math.log(1e-1)))
    return {
        "x": nrm(ks[0], (BATCH, SEQ, D_MODEL), jnp.float32),
        "meta_tokens": nrm(ks[1], (N_META, D_MODEL), jnp.float32),
        "norm_w": 1.0 + 0.02 * nrm(ks[2], (DEPTH, D_MODEL), jnp.float32),
        "w_in": nrm(ks[3], (DEPTH, D_MODEL, N_IN), jnp.float32) * D_MODEL ** -0.5,
        "w_out": nrm(ks[4], (DEPTH, D_INNER, D_MODEL), jnp.float32) * D_INNER ** -0.5,
        "attn_lambda": 0.1 * nrm(ks[5], (DEPTH, 4, ATTN_QK_DIM), jnp.float32),
        "attn_norm_w": 1.0 + 0.02 * nrm(ks[6], (DEPTH, ATTN_V_DIM), jnp.float32),
        "sconv_w": nrm(ks[7], (DEPTH, SCONV_WIDTH, D_CONV), jnp.float32) * SCONV_WIDTH ** -0.5,
        "ssm_conv_w": nrm(ks[8], (DEPTH, SSM_CONV_WIDTH, SSM_CONV_DIM), jnp.float32) * SSM_CONV_WIDTH ** -0.5,
        "ssm_conv_b": 0.02 * nrm(ks[9], (DEPTH, SSM_CONV_DIM), jnp.float32),
        "ssm_a_log": jnp.log(jax.random.uniform(ks[10], (DEPTH, SSM_HEADS), minval=1.0, maxval=16.0)),
        "ssm_dt_bias": dt0 + jnp.log(-jnp.expm1(-dt0)),
        "ssm_d": 1.0 + 0.02 * nrm(ks[12], (DEPTH, SSM_HEADS), jnp.float32),
        "ssm_norm_w": 1.0 + 0.02 * nrm(ks[13], (DEPTH, D_SSM), jnp.float32),
        "final_norm_w": 1.0 + 0.02 * nrm(ks[14], (D_MODEL,), jnp.float32),
    }


def reference(x, meta_tokens, norm_w, w_in, w_out, attn_lambda, attn_norm_w, sconv_w,
              ssm_conv_w, ssm_conv_b, ssm_a_log, ssm_dt_bias, ssm_d, ssm_norm_w, final_norm_w):
    b = x.shape[0]
    meta = jnp.broadcast_to(meta_tokens.astype(x.dtype)[None], (b, N_META, D_MODEL))
    h = jnp.concatenate([meta, x], axis=1)
    for layer in range(DEPTH):
        u = rmsnorm(h, norm_w[layer])
        proj = jnp.einsum('bld,de->ble', u, w_in[layer])
        (q, k, v, g_attn, c_b, c_c, c_h, g_conv, z, xbc, dt) = jnp.split(proj, IN_SPLIT_POINTS, axis=-1)
        y_attn = diff_attention_branch(q, k, v, attn_lambda[layer], attn_norm_w[layer], layer) * jax.nn.silu(g_attn)
        y_conv = short_conv_branch(c_b, c_c, c_h, sconv_w[layer]) * jax.nn.silu(g_conv)
        y_ssm = ssd_branch(z, xbc, dt, ssm_conv_w[layer], ssm_conv_b[layer], ssm_a_log[layer],
                           ssm_dt_bias[layer], ssm_d[layer], ssm_norm_w[layer])
        y = jnp.concatenate([y_attn, y_conv, y_ssm], axis=-1)
        h = h + jnp.einsum('ble,ed->bld', y, w_out[layer])
    return rmsnorm(h, final_norm_w)[:, N_META:]
```

```python
import functools
import math

import jax
import jax.numpy as jnp
from jax import lax
from jax.experimental import pallas as pl
from jax.experimental.pallas import tpu as pltpu

D_MODEL = 1024
DEPTH = 4
N_META = 16
D_ATTN = 512
D_CONV = 512
D_SSM = 1024
ATTN_HEADS = 4
ATTN_QK_DIM = 64
ATTN_V_DIM = 128
SCONV_WIDTH = 3
SSM_HEAD_DIM = 64
SSM_HEADS = 16
SSM_GROUPS = 4
SSM_STATE = 128
SSM_CONV_WIDTH = 4
SSM_CONV_DIM = 2048
CHUNK = 128
N_PAD = CHUNK - N_META
NORM_EPS = 1e-5
MASK_VALUE = -1e30
GROUP_W = D_SSM // SSM_GROUPS
HEADS_PER_GROUP = SSM_HEADS // SSM_GROUPS

OFF_QKV = 0
OFF_GATE_A = 3 * D_ATTN
OFF_SC = 4 * D_ATTN
OFF_Z = OFF_SC + 4 * D_CONV
OFF_XBC = OFF_Z + D_SSM
OFF_DT = OFF_XBC + SSM_CONV_DIM

TAIL = 8
MAIN_TILE = 256
VMEM_LIMIT_BYTES = 56 * 1024 * 1024

F32 = jnp.float32
BF16 = jnp.bfloat16
NT_DIMS = (((1,), (1,)), ((), ()))


def _silu(x):
    return x * (1.0 / (1.0 + jnp.exp(-x)))


def _softplus(x):
    return jnp.maximum(x, 0.0) + jnp.log1p(jnp.exp(-jnp.abs(x)))


def _rms(x, w):
    ms = jnp.mean(x * x, axis=-1, keepdims=True)
    return x * lax.rsqrt(ms + NORM_EPS) * w


def _attn_block(carry, qs, kblk, vblk, mask):
    out = []
    for qm, (m, l, acc) in zip(qs, carry):
        s = lax.dot_general(qm, kblk, NT_DIMS, preferred_element_type=F32)
        if mask is not None:
            s = jnp.where(mask, s, MASK_VALUE)
        m_new = jnp.maximum(m, jnp.max(s, axis=-1, keepdims=True))
        alpha = jnp.exp(m - m_new)
        p = jnp.exp(s - m_new)
        l_new = alpha * l + jnp.sum(p, axis=-1, keepdims=True)
        acc_new = alpha * acc + jnp.dot(p.astype(BF16), vblk, preferred_element_type=F32)
        out.append((m_new, l_new, acc_new))
    return tuple(out)


def _layer_kernel(h_ref, nw_ref, win_ref, wdt_ref, wdtT_ref, wout_ref, lam_ref, anw_ref,
                  scw_ref, xw_ref, xb_ref, alog_row_ref, alog_col_ref, dtb_row_ref, dtb_col_ref,
                  drep_ref, snw_ref, fnw_ref, kpre_ref, vpre_ref, tsc_ref, tx_ref, s0_ref,
                  *rest, T, n_pad, has_prefix, emit_state, final_norm, lam_init):
    if emit_state:
        (hout_ref, kout_ref, vout_ref, tsc_out_ref, tx_out_ref, s_out_ref,
         k_scr, v_scr, ext_sc, ext_x, s_scr, y_scr) = rest
    else:
        hout_ref, k_scr, v_scr, ext_sc, ext_x, s_scr, y_scr = rest

    i = pl.program_id(1)
    prefix = CHUNK if has_prefix else 0
    Q = CHUNK

    @pl.when(i == 0)
    def _init():
        if has_prefix:
            k_scr[0:CHUNK, :] = kpre_ref[...]
            v_scr[0:CHUNK, :] = vpre_ref[...]
        ext_sc[0:TAIL, :] = tsc_ref[...]
        ext_x[0:TAIL, :] = tx_ref[...]
        s_scr[...] = s0_ref[...]

    if n_pad:
        row_ok = lax.broadcasted_iota(jnp.int32, (T, 1), 0) >= n_pad
        col_ok = lax.broadcasted_iota(jnp.int32, (1, T), 1) >= n_pad

    h = h_ref[0]
    u = _rms(h, nw_ref[...])
    if n_pad:
        u = jnp.where(row_ok, u, 0.0)
    ub = u.astype(BF16)

    def proj(off, width):
        return jnp.dot(ub, win_ref[:, off:off + width], preferred_element_type=F32)

    qkv = proj(OFF_QKV, 3 * D_ATTN)
    q = qkv[:, 0:D_ATTN] * (ATTN_QK_DIM ** -0.5)
    kb = qkv[:, D_ATTN:2 * D_ATTN].astype(BF16)
    vb = qkv[:, 2 * D_ATTN:3 * D_ATTN].astype(BF16)
    base = pl.multiple_of(prefix + i * T, CHUNK)
    k_scr[pl.ds(base, T), :] = kb
    v_scr[pl.ds(base, T), :] = vb
    gate_a = proj(OFF_GATE_A, D_ATTN)

    lp = lam_ref[...]
    lam = (jnp.exp(jnp.sum(lp[0:1] * lp[1:2], axis=-1, keepdims=True))
           - jnp.exp(jnp.sum(lp[2:3] * lp[3:4], axis=-1, keepdims=True)) + lam_init)

    lane = lax.broadcasted_iota(jnp.int32, (T, ATTN_V_DIM), 1)
    rr = lax.broadcasted_iota(jnp.int32, (T, T), 0)
    cc = lax.broadcasted_iota(jnp.int32, (T, T), 1)
    diag_mask = cc <= rr
    if n_pad:
        diag_mask = jnp.logical_and(diag_mask, cc >= n_pad)
    prefix_mask = lax.broadcasted_iota(jnp.int32, (1, CHUNK), 1) >= N_PAD

    for hd in range(ATTN_HEADS):
        cols = slice(hd * ATTN_V_DIM, (hd + 1) * ATTN_V_DIM)
        qh = q[:, cols]
        qs = (jnp.where(lane < ATTN_QK_DIM, qh, 0.0).astype(BF16),
              jnp.where(lane >= ATTN_QK_DIM, qh, 0.0).astype(BF16))
        init = tuple((jnp.full((T, 1), -jnp.inf, F32), jnp.zeros((T, 1), F32),
                      jnp.zeros((T, ATTN_V_DIM), F32)) for _ in range(2))
        carry = init
        if has_prefix:
            carry = _attn_block(carry, qs, k_scr[0:CHUNK, cols], v_scr[0:CHUNK, cols], prefix_mask)

            def body(j, c, qs=qs, cols=cols):
                off = pl.multiple_of(prefix + j * T, CHUNK)
                return _attn_block(c, qs, k_scr[pl.ds(off, T), cols], v_scr[pl.ds(off, T), cols], None)

            carry = lax.fori_loop(0, i, body, carry)
        carry = _attn_block(carry, qs, kb[:, cols], vb[:, cols], diag_mask)
        (_, l0, acc0), (_, l1, acc1) = carry
        o = acc0 / l0 - lam * (acc1 / l1)
        o = _rms(o, anw_ref[...]) * (1.0 - lam_init)
        y_scr[:, cols] = (o * _silu(gate_a[:, cols])).astype(BF16)

    sc = proj(OFF_SC, 4 * D_CONV)
    c_b = sc[:, 0:D_CONV]
    cch = sc[:, D_CONV:2 * D_CONV] * sc[:, 2 * D_CONV:3 * D_CONV]
    g_conv = sc[:, 3 * D_CONV:4 * D_CONV]
    ext_sc[TAIL:TAIL + T, :] = cch
    conv = cch * scw_ref[SCONV_WIDTH - 1:SCONV_WIDTH, :]
    for j in range(SCONV_WIDTH - 1):
        shift = SCONV_WIDTH - 1 - j
        conv = conv + ext_sc[TAIL - shift:TAIL - shift + T, :] * scw_ref[j:j + 1, :]
    y_scr[:, D_ATTN:D_ATTN + D_CONV] = (c_b * conv * _silu(g_conv)).astype(BF16)
    ext_sc[0:TAIL, :] = ext_sc[T:T + TAIL, :]

    z = proj(OFF_Z, D_SSM)
    xbc_raw = proj(OFF_XBC, SSM_CONV_DIM)
    ext_x[TAIL:TAIL + T, :] = xbc_raw
    xconv = xbc_raw * xw_ref[SSM_CONV_WIDTH - 1:SSM_CONV_WIDTH, :] + xb_ref[...]
    for j in range(SSM_CONV_WIDTH - 1):
        shift = SSM_CONV_WIDTH - 1 - j
        xconv = xconv + ext_x[TAIL - shift:TAIL - shift + T, :] * xw_ref[j:j + 1, :]
    ext_x[0:TAIL, :] = ext_x[T:T + TAIL, :]
    xbc = _silu(xconv)
    if n_pad:
        xbc = jnp.where(row_ok, xbc, 0.0)

    dt = _softplus(jnp.dot(ub, wdt_ref[...], preferred_element_type=F32) + dtb_row_ref[...])
    dtT = _softplus(lax.dot_general(wdtT_ref[...], ub, NT_DIMS, preferred_element_type=F32)
                    + dtb_col_ref[...])
    if n_pad:
        dt = jnp.where(row_ok, dt, 0.0)
        dtT = jnp.where(col_ok, dtT, 0.0)
    adt = dt * (-jnp.exp(alog_row_ref[...]))
    adtT = dtT * (-jnp.exp(alog_col_ref[...]))

    qr = lax.broadcasted_iota(jnp.int32, (Q, Q), 0)
    qc = lax.broadcasted_iota(jnp.int32, (Q, Q), 1)
    tril = qr >= qc
    lower_ones = tril.astype(F32)
    upper_ones = (qr <= qc).astype(F32)
    expand = (lax.broadcasted_iota(jnp.int32, (CHUNK, D_SSM), 0)
              == (lax.broadcasted_iota(jnp.int32, (CHUNK, D_SSM), 1) // SSM_HEAD_DIM)).astype(F32)
    colblk = lax.broadcasted_iota(jnp.int32, (Q, GROUP_W), 1) // SSM_HEAD_DIM

    for c in range(T // Q):
        rows = slice(c * Q, (c + 1) * Q)
        acum = jnp.dot(lower_ones, adt[rows], preferred_element_type=F32,
                       precision=lax.Precision.HIGHEST)
        acumT = jnp.dot(adtT[:, rows], upper_ones, preferred_element_type=F32,
                        precision=lax.Precision.HIGHEST)
        dtT_c = dtT[:, rows]
        wT = jnp.exp(acumT[:, Q - 1:Q] - acumT) * dtT_c
        total8 = jnp.broadcast_to(acum[Q - 1:Q, :], (8, CHUNK))
        chunk_decay = jnp.exp(jnp.dot(total8, expand, preferred_element_type=F32,
                                      precision=lax.Precision.HIGHEST))[0:1, :]
        xs_c = xbc[rows, 0:D_SSM]
        ys = []
        for g in range(SSM_GROUPS):
            gcols = slice(g * GROUP_W, (g + 1) * GROUP_W)
            b_g = xbc[rows, D_SSM + g * SSM_STATE:D_SSM + (g + 1) * SSM_STATE]
            c_g = xbc[rows, D_SSM + SSM_GROUPS * SSM_STATE + g * SSM_STATE:
                      D_SSM + SSM_GROUPS * SSM_STATE + (g + 1) * SSM_STATE]
            cb = lax.dot_general(c_g.astype(BF16), b_g.astype(BF16), NT_DIMS, preferred_element_type=F32)
            b_gT = b_g.T
            xg = xs_c[:, gcols]
            s_prev = s_scr[g]
            x_bd = jnp.concatenate([jnp.where(colblk == hh, xg, 0.0).astype(BF16)
                                    for hh in range(HEADS_PER_GROUP)], axis=0)
            s_bd = jnp.concatenate([jnp.where(colblk == hh, s_prev, 0.0).astype(BF16)
                                    for hh in range(HEADS_PER_GROUP)], axis=0)
            ms, ces, bws = [], [], []
            for hh in range(HEADS_PER_GROUP):
                hidx = g * HEADS_PER_GROUP + hh
                acol = jnp.broadcast_to(acum[:, hidx:hidx + 1], (Q, Q))
                arow = acumT[hidx:hidx + 1, :]
                decay = jnp.exp(jnp.where(tril, acol - arow, MASK_VALUE))
                ms.append((cb * decay * dtT_c[hidx:hidx + 1, :]).astype(BF16))
                ces.append((c_g * jnp.exp(acol)).astype(BF16))
                bws.append((b_gT * wT[hidx:hidx + 1, :]).astype(BF16))
            lhs = jnp.concatenate(ms + ces, axis=1)
            rhs = jnp.concatenate([x_bd, s_bd], axis=0)
            ys.append(jnp.dot(lhs, rhs, preferred_element_type=F32))
            s_new = chunk_decay[:, gcols] * s_prev + jnp.dot(jnp.concatenate(bws, axis=1), x_bd,
                                                             preferred_element_type=F32)
            s_scr[g] = s_new
        y_c = jnp.concatenate(ys, axis=1) + xs_c * drep_ref[...]
        ug = y_c * _silu(z[rows])
        parts = []
        for g in range(SSM_GROUPS):
            blk = ug[:, g * GROUP_W:(g + 1) * GROUP_W]
            ms_g = jnp.mean(blk * blk, axis=-1, keepdims=True)
            parts.append(blk * lax.rsqrt(ms_g + NORM_EPS))
        yn = jnp.concatenate(parts, axis=1) * snw_ref[...]
        y_scr[rows, D_ATTN + D_CONV:D_ATTN + D_CONV + D_SSM] = yn.astype(BF16)

    out = h + jnp.dot(y_scr[...], wout_ref[...], preferred_element_type=F32)
    if final_norm:
        out = _rms(out, fnw_ref[...])
    hout_ref[0] = out

    if emit_state:
        kout_ref[...] = kb
        vout_ref[...] = vb
        tsc_out_ref[...] = ext_sc[0:TAIL, :]
        tx_out_ref[...] = ext_x[0:TAIL, :]
        s_out_ref[...] = s_scr[...]


def _const_spec(shape):
    nd = len(shape)
    return pl.BlockSpec(shape, lambda b, i, _nd=nd: (0,) * _nd, pipeline_mode=pl.Buffered(1))


def _layer_call(h, params, state, *, T, n_pad, has_prefix, emit_state, final_norm, lam_init):
    bsz, seq, _ = h.shape
    nt = seq // T
    n_keys = (CHUNK if has_prefix else 0) + seq
    kern = functools.partial(_layer_kernel, T=T, n_pad=n_pad, has_prefix=has_prefix,
                             emit_state=emit_state, final_norm=final_norm, lam_init=lam_init)
    ins = [h] + list(params) + list(state)
    in_specs = [pl.BlockSpec((1, T, D_MODEL), lambda b, i: (b, i, 0))]
    in_specs += [_const_spec(a.shape) for a in list(params) + list(state)]
    out_shape = [jax.ShapeDtypeStruct((bsz, seq, D_MODEL), F32)]
    out_specs = [pl.BlockSpec((1, T, D_MODEL), lambda b, i: (b, i, 0))]
    if emit_state:
        st_shapes = [((T, D_ATTN), BF16), ((T, D_ATTN), BF16), ((TAIL, D_CONV), F32),
                     ((TAIL, SSM_CONV_DIM), F32), ((SSM_GROUPS, SSM_STATE, GROUP_W), F32)]
        for shp, dt_ in st_shapes:
            out_shape.append(jax.ShapeDtypeStruct(shp, dt_))
            out_specs.append(pl.BlockSpec(shp, lambda b, i, _nd=len(shp): (0,) * _nd))
    scratch = [
        pltpu.VMEM((n_keys, D_ATTN), BF16),
        pltpu.VMEM((n_keys, D_ATTN), BF16),
        pltpu.VMEM((TAIL + T, D_CONV), F32),
        pltpu.VMEM((TAIL + T, SSM_CONV_DIM), F32),
        pltpu.VMEM((SSM_GROUPS, SSM_STATE, GROUP_W), F32),
        pltpu.VMEM((T, 2 * D_MODEL), BF16),
    ]
    return pl.pallas_call(
        kern,
        grid=(bsz, nt),
        in_specs=in_specs,
        out_specs=out_specs,
        out_shape=out_shape,
        scratch_shapes=scratch,
        compiler_params=pltpu.CompilerParams(
            dimension_semantics=("arbitrary", "arbitrary"),
            vmem_limit_bytes=VMEM_LIMIT_BYTES),
        name="layer_meta" if emit_state else "layer_main",
    )(*ins)


def kernel(x, meta_tokens, norm_w, w_in, w_out, attn_lambda, attn_norm_w, sconv_w, ssm_conv_w,
           ssm_conv_b, ssm_a_log, ssm_dt_bias, ssm_d, ssm_norm_w, final_norm_w):
    bsz = x.shape[0]
    pad_heads = CHUNK - SSM_HEADS
    h_meta = jnp.concatenate([jnp.zeros((N_PAD, D_MODEL), F32), meta_tokens.astype(F32)], axis=0)[None]
    h_main = x.astype(F32)
    zero_state = (
        jnp.zeros((CHUNK, D_ATTN), BF16), jnp.zeros((CHUNK, D_ATTN), BF16),
        jnp.zeros((TAIL, D_CONV), F32), jnp.zeros((TAIL, SSM_CONV_DIM), F32),
        jnp.zeros((SSM_GROUPS, SSM_STATE, GROUP_W), F32),
    )
    fnw = final_norm_w.astype(F32)[None, :]
    for layer in range(DEPTH):
        lam_init = 0.8 - 0.6 * math.exp(-0.3 * layer)
        w_l = w_in[layer]
        w_dt = w_l[:, OFF_DT:OFF_DT + SSM_HEADS]
        params = (
            norm_w[layer].astype(F32)[None, :],
            w_l[:, :OFF_DT].astype(BF16),
            jnp.pad(w_dt, ((0, 0), (0, pad_heads))).astype(BF16),
            w_dt.T.astype(BF16),
            w_out[layer].astype(BF16),
            attn_lambda[layer].astype(F32),
            attn_norm_w[layer].astype(F32)[None, :],
            sconv_w[layer].astype(F32),
            ssm_conv_w[layer].astype(F32),
            ssm_conv_b[layer].astype(F32)[None, :],
            jnp.pad(ssm_a_log[layer].astype(F32), (0, pad_heads), constant_values=-jnp.inf)[None, :],
            ssm_a_log[layer].astype(F32)[:, None],
            jnp.pad(ssm_dt_bias[layer].astype(F32), (0, pad_heads))[None, :],
            ssm_dt_bias[layer].astype(F32)[:, None],
            jnp.repeat(ssm_d[layer].astype(F32), SSM_HEAD_DIM)[None, :],
            ssm_norm_w[layer].astype(F32)[None, :],
            fnw,
        )
        meta_out = _layer_call(h_meta, params, zero_state, T=CHUNK, n_pad=N_PAD, has_prefix=False,
                               emit_state=True, final_norm=False, lam_init=lam_init)
        h_meta, meta_state = meta_out[0], tuple(meta_out[1:])
        (h_main,) = _layer_call(h_main, params, meta_state, T=MAIN_TILE, n_pad=0, has_prefix=True,
                                emit_state=False, final_norm=(layer == DEPTH - 1), lam_init=lam_init)
    return h_main.astype(x.dtype)
```

```python
import functools
import math

import jax
import jax.numpy as jnp
from jax import lax
from jax.experimental import pallas as pl
from jax.experimental.pallas import tpu as pltpu

D_MODEL = 1024
DEPTH = 4
N_META = 16
D_ATTN = 512
D_CONV = 512
D_SSM = 1024
ATTN_HEADS = 4
ATTN_QK_DIM = 64
ATTN_V_DIM = 128
SCONV_WIDTH = 3
SSM_HEAD_DIM = 64
SSM_HEADS = 16
SSM_GROUPS = 4
SSM_STATE = 128
SSM_CONV_WIDTH = 4
SSM_CONV_DIM = 2048
CHUNK = 128
N_PAD = CHUNK - N_META
NORM_EPS = 1e-5
MASK_VALUE = -1e30
GROUP_W = D_SSM // SSM_GROUPS
HEADS_PER_GROUP = SSM_HEADS // SSM_GROUPS

OFF_QKV = 0
OFF_GATE_A = 3 * D_ATTN
OFF_SC = 4 * D_ATTN
OFF_Z = OFF_SC + 4 * D_CONV
OFF_XBC = OFF_Z + D_SSM
OFF_DT = OFF_XBC + SSM_CONV_DIM

TAIL = 8
MAIN_TILE = 256
VMEM_LIMIT_BYTES = 56 * 1024 * 1024

F32 = jnp.float32
BF16 = jnp.bfloat16
NT_DIMS = (((1,), (1,)), ((), ()))


def _silu(x):
    return x * (1.0 / (1.0 + jnp.exp(-x)))


def _softplus(x):
    return jnp.maximum(x, 0.0) + jnp.log1p(jnp.exp(-jnp.abs(x)))


def _rms(x, w):
    ms = jnp.mean(x * x, axis=-1, keepdims=True)
    return x * lax.rsqrt(ms + NORM_EPS) * w


def _attn_step(stats, q_scr, acc_scr, k_blk_fn, vT_blk_fn, mask, first, T):
    new = []
    for hd in range(ATTN_HEADS):
        s2 = lax.dot_general(k_blk_fn(hd), q_scr[hd], NT_DIMS, preferred_element_type=F32)
        vT_blk = vT_blk_fn(hd)
        for mp in range(2):
            c = 2 * hd + mp
            s = s2[:, mp * T:(mp + 1) * T]
            if mask is not None:
                s = jnp.where(mask, s, MASK_VALUE)
            bm = jnp.max(s, axis=0, keepdims=True)
            if first:
                m_new = bm
                p = jnp.exp(s - m_new)
                l_new = jnp.sum(p, axis=0, keepdims=True)
                acc_scr[c] = jnp.dot(vT_blk, p.astype(BF16), preferred_element_type=F32)
            else:
                m, l = stats[c]
                m_new = jnp.maximum(m, bm)
                alpha = jnp.exp(m - m_new)
                p = jnp.exp(s - m_new)
                l_new = alpha * l + jnp.sum(p, axis=0, keepdims=True)
                acc_scr[c] = alpha * acc_scr[c] + jnp.dot(vT_blk, p.astype(BF16), preferred_element_type=F32)
            new.append((m_new, l_new))
    return tuple(new)


def _layer_kernel(h_ref, nw_ref, win_ref, wdt_ref, wdtT_ref, wout_ref, lam_ref, anw_ref,
                  scw_ref, xw_ref, xb_ref, alog_row_ref, alog_col_ref, dtb_row_ref, dtb_col_ref,
                  drep_ref, snw_ref, fnw_ref, kpre_ref, vpreT_ref, tsc_ref, tx_ref, s0_ref,
                  *rest, T, n_pad, has_prefix, emit_state, final_norm, lam_init):
    if emit_state:
        (hout_ref, kout_ref, vTout_ref, tsc_out_ref, tx_out_ref, s_out_ref,
         k_scr, vT_scr, q_scr, acc_scr, ext_sc, ext_x, s_scr, y_scr) = rest
    else:
        hout_ref, k_scr, vT_scr, q_scr, acc_scr, ext_sc, ext_x, s_scr, y_scr = rest

    i = pl.program_id(1)
    Q = CHUNK

    @pl.when(i == 0)
    def _init():
        ext_sc[0:TAIL, :] = tsc_ref[...]
        ext_x[0:TAIL, :] = tx_ref[...]
        s_scr[...] = s0_ref[...]

    if n_pad:
        row_ok = lax.broadcasted_iota(jnp.int32, (T, 1), 0) >= n_pad
        col_ok = lax.broadcasted_iota(jnp.int32, (1, T), 1) >= n_pad

    h = h_ref[0]
    u = _rms(h, nw_ref[...])
    if n_pad:
        u = jnp.where(row_ok, u, 0.0)
    ub = u.astype(BF16)

    def proj(off, width):
        return jnp.dot(ub, win_ref[:, off:off + width], preferred_element_type=F32)

    qkv = proj(OFF_QKV, 3 * D_ATTN)
    q = qkv[:, 0:D_ATTN] * (ATTN_QK_DIM ** -0.5)
    kb = qkv[:, D_ATTN:2 * D_ATTN].astype(BF16)
    vTb = qkv[:, 2 * D_ATTN:3 * D_ATTN].T.astype(BF16)
    k_scr[pl.ds(pl.multiple_of(i * T, CHUNK), T), :] = kb
    vT_scr[i] = vTb
    gate_a = proj(OFF_GATE_A, D_ATTN)

    lp = lam_ref[...]
    lam = (jnp.exp(jnp.sum(lp[0:1] * lp[1:2], axis=-1, keepdims=True))
           - jnp.exp(jnp.sum(lp[2:3] * lp[3:4], axis=-1, keepdims=True)) + lam_init)

    lane = lax.broadcasted_iota(jnp.int32, (T, ATTN_V_DIM), 1)
    for hd in range(ATTN_HEADS):
        qh = q[:, hd * ATTN_V_DIM:(hd + 1) * ATTN_V_DIM]
        q_scr[hd, 0:T, :] = jnp.where(lane < ATTN_QK_DIM, qh, 0.0).astype(BF16)
        q_scr[hd, T:2 * T, :] = jnp.where(lane >= ATTN_QK_DIM, qh, 0.0).astype(BF16)

    def head_cols(hd):
        return slice(hd * ATTN_V_DIM, (hd + 1) * ATTN_V_DIM)

    key_i = lax.broadcasted_iota(jnp.int32, (T, T), 0)
    qry_i = lax.broadcasted_iota(jnp.int32, (T, T), 1)
    diag_mask = key_i <= qry_i
    if n_pad:
        diag_mask = jnp.logical_and(diag_mask, key_i >= n_pad)

    stats = None
    if has_prefix:
        prefix_mask = lax.broadcasted_iota(jnp.int32, (CHUNK, T), 0) >= N_PAD
        stats = _attn_step(None, q_scr, acc_scr,
                           lambda hd: kpre_ref[:, head_cols(hd)],
                           lambda hd: vpreT_ref[head_cols(hd), :],
                           prefix_mask, True, T)

        def body(j, st):
            off = pl.multiple_of(j * T, CHUNK)
            return _attn_step(st, q_scr, acc_scr,
                              lambda hd: k_scr[pl.ds(off, T), head_cols(hd)],
                              lambda hd: vT_scr[j, head_cols(hd), :],
                              None, False, T)

        stats = lax.fori_loop(0, i, body, stats)
    stats = _attn_step(stats, q_scr, acc_scr,
                       lambda hd: kb[:, head_cols(hd)],
                       lambda hd: vTb[head_cols(hd), :],
                       diag_mask, not has_prefix, T)

    for hd in range(ATTN_HEADS):
        (_, l0), (_, l1) = stats[2 * hd], stats[2 * hd + 1]
        oT = acc_scr[2 * hd] * (1.0 / l0) - lam * (acc_scr[2 * hd + 1] * (1.0 / l1))
        ms = jnp.mean(oT * oT, axis=0, keepdims=True)
        oT = oT * lax.rsqrt(ms + NORM_EPS) * anw_ref[...] * (1.0 - lam_init)
        y_scr[:, head_cols(hd)] = (oT.T * _silu(gate_a[:, head_cols(hd)])).astype(BF16)

    sc = proj(OFF_SC, 4 * D_CONV)
    c_b = sc[:, 0:D_CONV]
    cch = sc[:, D_CONV:2 * D_CONV] * sc[:, 2 * D_CONV:3 * D_CONV]
    g_conv = sc[:, 3 * D_CONV:4 * D_CONV]
    ext_sc[TAIL:TAIL + T, :] = cch
    conv = cch * scw_ref[SCONV_WIDTH - 1:SCONV_WIDTH, :]
    for j in range(SCONV_WIDTH - 1):
        shift = SCONV_WIDTH - 1 - j
        conv = conv + ext_sc[TAIL - shift:TAIL - shift + T, :] * scw_ref[j:j + 1, :]
    y_scr[:, D_ATTN:D_ATTN + D_CONV] = (c_b * conv * _silu(g_conv)).astype(BF16)
    ext_sc[0:TAIL, :] = ext_sc[T:T + TAIL, :]

    z = proj(OFF_Z, D_SSM)
    xbc_raw = proj(OFF_XBC, SSM_CONV_DIM)
    ext_x[TAIL:TAIL + T, :] = xbc_raw
    xconv = xbc_raw * xw_ref[SSM_CONV_WIDTH - 1:SSM_CONV_WIDTH, :] + xb_ref[...]
    for j in range(SSM_CONV_WIDTH - 1):
        shift = SSM_CONV_WIDTH - 1 - j
        xconv = xconv + ext_x[TAIL - shift:TAIL - shift + T, :] * xw_ref[j:j + 1, :]
    ext_x[0:TAIL, :] = ext_x[T:T + TAIL, :]
    xbc = _silu(xconv)
    if n_pad:
        xbc = jnp.where(row_ok, xbc, 0.0)

    dt = _softplus(jnp.dot(ub, wdt_ref[...], preferred_element_type=F32) + dtb_row_ref[...])
    dtT = _softplus(lax.dot_general(wdtT_ref[...], ub, NT_DIMS, preferred_element_type=F32)
                    + dtb_col_ref[...])
    if n_pad:
        dt = jnp.where(row_ok, dt, 0.0)
        dtT = jnp.where(col_ok, dtT, 0.0)
    adt = dt * (-jnp.exp(alog_row_ref[...]))
    adtT = dtT * (-jnp.exp(alog_col_ref[...]))

    qr = lax.broadcasted_iota(jnp.int32, (Q, Q), 0)
    qc = lax.broadcasted_iota(jnp.int32, (Q, Q), 1)
    tril = qr >= qc
    lower_ones = tril.astype(F32)
    upper_ones = (qr <= qc).astype(F32)
    expand = (lax.broadcasted_iota(jnp.int32, (CHUNK, D_SSM), 0)
              == (lax.broadcasted_iota(jnp.int32, (CHUNK, D_SSM), 1) // SSM_HEAD_DIM)).astype(F32)
    colblk = lax.broadcasted_iota(jnp.int32, (Q, GROUP_W), 1) // SSM_HEAD_DIM

    for c in range(T // Q):
        rows = slice(c * Q, (c + 1) * Q)
        acum = jnp.dot(lower_ones, adt[rows], preferred_element_type=F32,
                       precision=lax.Precision.HIGHEST)
        acumT = jnp.dot(adtT[:, rows], upper_ones, preferred_element_type=F32,
                        precision=lax.Precision.HIGHEST)
        dtT_c = dtT[:, rows]
        wT = jnp.exp(acumT[:, Q - 1:Q] - acumT) * dtT_c
        total8 = jnp.broadcast_to(acum[Q - 1:Q, :], (8, CHUNK))
        chunk_decay = jnp.exp(jnp.dot(total8, expand, preferred_element_type=F32,
                                      precision=lax.Precision.HIGHEST))[0:1, :]
        xs_c = xbc[rows, 0:D_SSM]
        ys = []
        for g in range(SSM_GROUPS):
            gcols = slice(g * GROUP_W, (g + 1) * GROUP_W)
            b_g = xbc[rows, D_SSM + g * SSM_STATE:D_SSM + (g + 1) * SSM_STATE]
            c_g = xbc[rows, D_SSM + SSM_GROUPS * SSM_STATE + g * SSM_STATE:
                      D_SSM + SSM_GROUPS * SSM_STATE + (g + 1) * SSM_STATE]
            cb = lax.dot_general(c_g.astype(BF16), b_g.astype(BF16), NT_DIMS, preferred_element_type=F32)
            b_gT = b_g.T
            xg = xs_c[:, gcols]
            s_prev = s_scr[g]
            x_bd = jnp.concatenate([jnp.where(colblk == hh, xg, 0.0).astype(BF16)
                                    for hh in range(HEADS_PER_GROUP)], axis=0)
            s_bd = jnp.concatenate([jnp.where(colblk == hh, s_prev, 0.0).astype(BF16)
                                    for hh in range(HEADS_PER_GROUP)], axis=0)
            ms, ces, bws = [], [], []
            for hh in range(HEADS_PER_GROUP):
                hidx = g * HEADS_PER_GROUP + hh
                acol = jnp.broadcast_to(acum[:, hidx:hidx + 1], (Q, Q))
                arow = acumT[hidx:hidx + 1, :]
                decay = jnp.exp(jnp.where(tril, acol - arow, MASK_VALUE))
                ms.append((cb * decay * dtT_c[hidx:hidx + 1, :]).astype(BF16))
                ces.append((c_g * jnp.exp(acol)).astype(BF16))
                bws.append((b_gT * wT[hidx:hidx + 1, :]).astype(BF16))
            lhs = jnp.concatenate(ms + ces, axis=1)
            rhs = jnp.concatenate([x_bd, s_bd], axis=0)
            ys.append(jnp.dot(lhs, rhs, preferred_element_type=F32))
            s_new = chunk_decay[:, gcols] * s_prev + jnp.dot(jnp.concatenate(bws, axis=1), x_bd,
                                                             preferred_element_type=F32)
            s_scr[g] = s_new
        y_c = jnp.concatenate(ys, axis=1) + xs_c * drep_ref[...]
        ug = y_c * _silu(z[rows])
        parts = []
        for g in range(SSM_GROUPS):
            blk = ug[:, g * GROUP_W:(g + 1) * GROUP_W]
            ms_g = jnp.mean(blk * blk, axis=-1, keepdims=True)
            parts.append(blk * lax.rsqrt(ms_g + NORM_EPS))
        yn = jnp.concatenate(parts, axis=1) * snw_ref[...]
        y_scr[rows, D_ATTN + D_CONV:D_ATTN + D_CONV + D_SSM] = yn.astype(BF16)

    out = h + jnp.dot(y_scr[...], wout_ref[...], preferred_element_type=F32)
    if final_norm:
        out = _rms(out, fnw_ref[...])
    hout_ref[0] = out

    if emit_state:
        kout_ref[...] = kb
        vTout_ref[...] = vTb
        tsc_out_ref[...] = ext_sc[0:TAIL, :]
        tx_out_ref[...] = ext_x[0:TAIL, :]
        s_out_ref[...] = s_scr[...]


def _const_spec(shape):
    nd = len(shape)
    return pl.BlockSpec(shape, lambda b, i, _nd=nd: (0,) * _nd, pipeline_mode=pl.Buffered(1))


def _layer_call(h, params, state, *, T, n_pad, has_prefix, emit_state, final_norm, lam_init):
    bsz, seq, _ = h.shape
    nt = seq // T
    kern = functools.partial(_layer_kernel, T=T, n_pad=n_pad, has_prefix=has_prefix,
                             emit_state=emit_state, final_norm=final_norm, lam_init=lam_init)
    ins = [h] + list(params) + list(state)
    in_specs = [pl.BlockSpec((1, T, D_MODEL), lambda b, i: (b, i, 0))]
    in_specs += [_const_spec(a.shape) for a in list(params) + list(state)]
    out_shape = [jax.ShapeDtypeStruct((bsz, seq, D_MODEL), F32)]
    out_specs = [pl.BlockSpec((1, T, D_MODEL), lambda b, i: (b, i, 0))]
    if emit_state:
        st_shapes = [((T, D_ATTN), BF16), ((D_ATTN, T), BF16), ((TAIL, D_CONV), F32),
                     ((TAIL, SSM_CONV_DIM), F32), ((SSM_GROUPS, SSM_STATE, GROUP_W), F32)]
        for shp, dt_ in st_shapes:
            out_shape.append(jax.ShapeDtypeStruct(shp, dt_))
            out_specs.append(pl.BlockSpec(shp, lambda b, i, _nd=len(shp): (0,) * _nd))
    scratch = [
        pltpu.VMEM((seq, D_ATTN), BF16),
        pltpu.VMEM((nt, D_ATTN, T), BF16),
        pltpu.VMEM((ATTN_HEADS, 2 * T, ATTN_V_DIM), BF16),
        pltpu.VMEM((2 * ATTN_HEADS, ATTN_V_DIM, T), F32),
        pltpu.VMEM((TAIL + T, D_CONV), F32),
        pltpu.VMEM((TAIL + T, SSM_CONV_DIM), F32),
        pltpu.VMEM((SSM_GROUPS, SSM_STATE, GROUP_W), F32),
        pltpu.VMEM((T, 2 * D_MODEL), BF16),
    ]
    return pl.pallas_call(
        kern,
        grid=(bsz, nt),
        in_specs=in_specs,
        out_specs=out_specs,
        out_shape=out_shape,
        scratch_shapes=scratch,
        compiler_params=pltpu.CompilerParams(
            dimension_semantics=("arbitrary", "arbitrary"),
            vmem_limit_bytes=VMEM_LIMIT_BYTES),
        name="layer_meta" if emit_state else "layer_main",
    )(*ins)


def kernel(x, meta_tokens, norm_w, w_in, w_out, attn_lambda, attn_norm_w, sconv_w, ssm_conv_w,
           ssm_conv_b, ssm_a_log, ssm_dt_bias, ssm_d, ssm_norm_w, final_norm_w):
    pad_heads = CHUNK - SSM_HEADS
    h_meta = jnp.concatenate([jnp.zeros((N_PAD, D_MODEL), F32), meta_tokens.astype(F32)], axis=0)[None]
    h_main = x.astype(F32)
    zero_state = (
        jnp.zeros((CHUNK, D_ATTN), BF16), jnp.zeros((D_ATTN, CHUNK), BF16),
        jnp.zeros((TAIL, D_CONV), F32), jnp.zeros((TAIL, SSM_CONV_DIM), F32),
        jnp.zeros((SSM_GROUPS, SSM_STATE, GROUP_W), F32),
    )
    fnw = final_norm_w.astype(F32)[None, :]
    for layer in range(DEPTH):
        lam_init = 0.8 - 0.6 * math.exp(-0.3 * layer)
        w_l = w_in[layer]
        w_dt = w_l[:, OFF_DT:OFF_DT + SSM_HEADS]
        params = (
            norm_w[layer].astype(F32)[None, :],
            w_l[:, :OFF_DT].astype(BF16),
            jnp.pad(w_dt, ((0, 0), (0, pad_heads))).astype(BF16),
            w_dt.T.astype(BF16),
            w_out[layer].astype(BF16),
            attn_lambda[layer].astype(F32),
            attn_norm_w[layer].astype(F32)[:, None],
            sconv_w[layer].astype(F32),
            ssm_conv_w[layer].astype(F32),
            ssm_conv_b[layer].astype(F32)[None, :],
            jnp.pad(ssm_a_log[layer].astype(F32), (0, pad_heads), constant_values=-jnp.inf)[None, :],
            ssm_a_log[layer].astype(F32)[:, None],
            jnp.pad(ssm_dt_bias[layer].astype(F32), (0, pad_heads))[None, :],
            ssm_dt_bias[layer].astype(F32)[:, None],
            jnp.repeat(ssm_d[layer].astype(F32), SSM_HEAD_DIM)[None, :],
            ssm_norm_w[layer].astype(F32)[None, :],
            fnw,
        )
        meta_out = _layer_call(h_meta, params, zero_state, T=CHUNK, n_pad=N_PAD, has_prefix=False,
                               emit_state=True, final_norm=False, lam_init=lam_init)
        h_meta, meta_state = meta_out[0], tuple(meta_out[1:])
        (h_main,) = _layer_call(h_main, params, meta_state, T=MAIN_TILE, n_pad=0, has_prefix=True,
                                emit_state=False, final_norm=(layer == DEPTH - 1), lam_init=lam_init)
    return h_main.astype(x.dtype)
```

```python
import functools
import math

import jax
import jax.numpy as jnp
from jax import lax
from jax.experimental import pallas as pl
from jax.experimental.pallas import tpu as pltpu

D_MODEL = 1024
DEPTH = 4
N_META = 16
D_ATTN = 512
D_CONV = 512
D_SSM = 1024
ATTN_HEADS = 4
ATTN_QK_DIM = 64
ATTN_V_DIM = 128
SCONV_WIDTH = 3
SSM_HEAD_DIM = 64
SSM_HEADS = 16
SSM_GROUPS = 4
SSM_STATE = 128
SSM_CONV_WIDTH = 4
SSM_CONV_DIM = 2048
CHUNK = 128
N_PAD = CHUNK - N_META
NORM_EPS = 1e-5
MASK_VALUE = -1e30
LOG2_E = 1.4426950408889634
V_ROWS = ATTN_V_DIM + 16
GROUP_W = D_SSM // SSM_GROUPS
HEADS_PER_GROUP = SSM_HEADS // SSM_GROUPS

OFF_QKV = 0
OFF_GATE_A = 3 * D_ATTN
OFF_SC = 4 * D_ATTN
OFF_Z = OFF_SC + 4 * D_CONV
OFF_XBC = OFF_Z + D_SSM
OFF_DT = OFF_XBC + SSM_CONV_DIM

TAIL = 8
MAIN_TILE = 256
VMEM_LIMIT_BYTES = 56 * 1024 * 1024

F32 = jnp.float32
BF16 = jnp.bfloat16
NT_DIMS = (((1,), (1,)), ((), ()))


def _silu(x):
    half = 0.5 * x
    return half * jnp.tanh(half) + half


def _softplus(x):
    return jnp.maximum(x, 0.0) + jnp.log1p(jnp.exp(-jnp.abs(x)))


def _rms(x, w):
    ms = jnp.mean(x * x, axis=-1, keepdims=True)
    return x * lax.rsqrt(ms + NORM_EPS) * w


def _attn_step(maxes, q_scr, acc_scr, k_blk_fn, vT_blk_fn, mask, first, T):
    s2s = [lax.dot_general(k_blk_fn(hd), q_scr[hd], NT_DIMS, preferred_element_type=F32)
           for hd in range(ATTN_HEADS)]
    new, ps, alphas = [], [], []
    for hd in range(ATTN_HEADS):
        for mp in range(2):
            s = s2s[hd][:, mp * T:(mp + 1) * T]
            if mask is not None:
                s = jnp.where(mask, s, MASK_VALUE)
            m_new = jnp.max(s, axis=0, keepdims=True)
            if not first:
                m_old = maxes[2 * hd + mp]
                m_new = jnp.maximum(m_old, m_new)
                alphas.append(jnp.exp2(m_old - m_new))
            ps.append(jnp.exp2(s - m_new).astype(BF16))
            new.append(m_new)
    for hd in range(ATTN_HEADS):
        pv = jnp.dot(vT_blk_fn(hd), jnp.concatenate(ps[2 * hd:2 * hd + 2], axis=1),
                     preferred_element_type=F32)
        if first:
            acc_scr[hd] = pv
        else:
            acc_scr[hd] = jnp.concatenate(alphas[2 * hd:2 * hd + 2], axis=1) * acc_scr[hd] + pv
    return tuple(new)


def _layer_kernel(h_ref, nw_ref, win_ref, wdt_ref, wdtT_ref, wout_ref, lam_ref, anw_ref,
                  scw_ref, xw_ref, xb_ref, alog_row_ref, alog_col_ref, dtb_row_ref, dtb_col_ref,
                  drep_ref, snw_ref, fnw_ref, kpre_ref, vpreT_ref, tsc_ref, tx_ref, s0_ref,
                  *rest, T, n_pad, has_prefix, emit_state, final_norm, lam_init):
    if emit_state:
        (hout_ref, kout_ref, vTout_ref, tsc_out_ref, tx_out_ref, s_out_ref,
         k_scr, vT_scr, q_scr, acc_scr, ext_sc, ext_x, s_scr) = rest
    else:
        hout_ref, k_scr, vT_scr, q_scr, acc_scr, ext_sc, ext_x, s_scr = rest

    i = pl.program_id(1)
    Q = CHUNK

    @pl.when(i == 0)
    def _init():
        ext_sc[0:TAIL, :] = tsc_ref[...]
        ext_x[0:TAIL, :] = tx_ref[...]
        s_scr[...] = s0_ref[...]

    if n_pad:
        row_ok = lax.broadcasted_iota(jnp.int32, (T, 1), 0) >= n_pad
        col_ok = lax.broadcasted_iota(jnp.int32, (1, T), 1) >= n_pad

    h = h_ref[0]
    u = _rms(h, nw_ref[...])
    if n_pad:
        u = jnp.where(row_ok, u, 0.0)
    ub = u.astype(BF16)

    def proj(off, width):
        return jnp.dot(ub, win_ref[:, off:off + width], preferred_element_type=F32)

    def out_proj(y, off):
        return jnp.dot(y.astype(BF16), wout_ref[off:off + y.shape[1], :], preferred_element_type=F32)

    def causal_conv(raw, ext_ref, cols, w_ref, width, bias):
        ext_ref[TAIL:TAIL + T, cols] = raw
        acc = raw * w_ref[width - 1:width, cols]
        if bias is not None:
            acc = acc + bias
        for j in range(width - 1):
            shift = width - 1 - j
            acc = acc + ext_ref[TAIL - shift:TAIL - shift + T, cols] * w_ref[j:j + 1, cols]
        ext_ref[0:TAIL, cols] = ext_ref[T:T + TAIL, cols]
        return acc

    def head_cols(hd):
        return slice(hd * ATTN_V_DIM, (hd + 1) * ATTN_V_DIM)

    qkv = proj(OFF_QKV, 3 * D_ATTN)
    sc = proj(OFF_SC, 4 * D_CONV)
    q = qkv[:, 0:D_ATTN] * (ATTN_QK_DIM ** -0.5 * LOG2_E)
    kb = qkv[:, D_ATTN:2 * D_ATTN].astype(BF16)
    vT = qkv[:, 2 * D_ATTN:3 * D_ATTN].T
    ones_rows = (lax.broadcasted_iota(jnp.int32, (V_ROWS - ATTN_V_DIM, T), 0) == 0).astype(BF16)
    vTb = [jnp.concatenate([vT[hd * ATTN_V_DIM:(hd + 1) * ATTN_V_DIM].astype(BF16), ones_rows], axis=0)
           for hd in range(ATTN_HEADS)]
    k_scr[pl.ds(pl.multiple_of(i * T, CHUNK), T), :] = kb
    for hd in range(ATTN_HEADS):
        vT_scr[i, hd] = vTb[hd]

    lane = lax.broadcasted_iota(jnp.int32, (T, ATTN_V_DIM), 1)
    for hd in range(ATTN_HEADS):
        qh = q[:, head_cols(hd)]
        q_scr[hd, 0:T, :] = jnp.where(lane < ATTN_QK_DIM, qh, 0.0).astype(BF16)
        q_scr[hd, T:2 * T, :] = jnp.where(lane >= ATTN_QK_DIM, qh, 0.0).astype(BF16)

    maxes = None
    if has_prefix:
        prefix_mask = lax.broadcasted_iota(jnp.int32, (CHUNK, T), 0) >= N_PAD
        maxes = _attn_step(None, q_scr, acc_scr,
                           lambda hd: kpre_ref[:, head_cols(hd)],
                           lambda hd: vpreT_ref[hd],
                           prefix_mask, True, T)

        def body(j, mx):
            off = pl.multiple_of(j * T, CHUNK)
            return _attn_step(mx, q_scr, acc_scr,
                              lambda hd: k_scr[pl.ds(off, T), head_cols(hd)],
                              lambda hd: vT_scr[j, hd],
                              None, False, T)

        maxes = lax.fori_loop(0, i, body, maxes)

    xs_raw = proj(OFF_XBC, D_SSM)
    key_i = lax.broadcasted_iota(jnp.int32, (T, T), 0)
    qry_i = lax.broadcasted_iota(jnp.int32, (T, T), 1)
    diag_mask = key_i <= qry_i
    if n_pad:
        diag_mask = jnp.logical_and(diag_mask, key_i >= n_pad)
    _attn_step(maxes, q_scr, acc_scr,
               lambda hd: kb[:, head_cols(hd)],
               lambda hd: vTb[hd],
               diag_mask, not has_prefix, T)

    cch = sc[:, D_CONV:2 * D_CONV] * sc[:, 2 * D_CONV:3 * D_CONV]
    conv = causal_conv(cch, ext_sc, slice(0, D_CONV), scw_ref, SCONV_WIDTH, None)
    y_conv = sc[:, 0:D_CONV] * conv * _silu(sc[:, 3 * D_CONV:4 * D_CONV])

    bc_raw = proj(OFF_XBC + D_SSM, SSM_CONV_DIM - D_SSM)
    xs = _silu(causal_conv(xs_raw, ext_x, slice(0, D_SSM), xw_ref, SSM_CONV_WIDTH, xb_ref[:, 0:D_SSM]))
    gate_a = proj(OFF_GATE_A, D_ATTN)
    out = h + out_proj(y_conv, D_ATTN)
    bc = _silu(causal_conv(bc_raw, ext_x, slice(D_SSM, SSM_CONV_DIM), xw_ref, SSM_CONV_WIDTH,
                           xb_ref[:, D_SSM:SSM_CONV_DIM]))
    if n_pad:
        xs = jnp.where(row_ok, xs, 0.0)
        bc = jnp.where(row_ok, bc, 0.0)
    z = proj(OFF_Z, D_SSM)

    lp = lam_ref[...]
    lam = (jnp.exp(jnp.sum(lp[0:1] * lp[1:2], axis=-1, keepdims=True))
           - jnp.exp(jnp.sum(lp[2:3] * lp[3:4], axis=-1, keepdims=True)) + lam_init)
    y_attn = []
    for hd in range(ATTN_HEADS):
        dv = ATTN_V_DIM
        inv_l0 = 1.0 / acc_scr[hd, dv:dv + 1, 0:T]
        inv_l1 = 1.0 / acc_scr[hd, dv:dv + 1, T:2 * T]
        oT = acc_scr[hd, 0:dv, 0:T] * inv_l0 - lam * (acc_scr[hd, 0:dv, T:2 * T] * inv_l1)
        ms = jnp.mean(oT * oT, axis=0, keepdims=True)
        oT = oT * lax.rsqrt(ms + NORM_EPS) * anw_ref[...] * (1.0 - lam_init)
        y_attn.append(oT.T * _silu(gate_a[:, head_cols(hd)]))
    out = out + out_proj(jnp.concatenate(y_attn, axis=1), 0)

    dt = _softplus(jnp.dot(ub, wdt_ref[...], preferred_element_type=F32) + dtb_row_ref[...])
    dtT = _softplus(lax.dot_general(wdtT_ref[...], ub, NT_DIMS, preferred_element_type=F32)
                    + dtb_col_ref[...])
    if n_pad:
        dt = jnp.where(row_ok, dt, 0.0)
        dtT = jnp.where(col_ok, dtT, 0.0)
    adt = dt * (-jnp.exp(alog_row_ref[...]))
    adtT = dtT * (-jnp.exp(alog_col_ref[...]))

    qr = lax.broadcasted_iota(jnp.int32, (Q, Q), 0)
    qc = lax.broadcasted_iota(jnp.int32, (Q, Q), 1)
    tril = qr >= qc
    lower_ones = tril.astype(F32)
    upper_ones = (qr <= qc).astype(F32)
    expand = (lax.broadcasted_iota(jnp.int32, (CHUNK, D_SSM), 0)
              == (lax.broadcasted_iota(jnp.int32, (CHUNK, D_SSM), 1) // SSM_HEAD_DIM)).astype(F32)
    colblk = lax.broadcasted_iota(jnp.int32, (Q, GROUP_W), 1) // SSM_HEAD_DIM
    head_mask = [jnp.where(colblk == hh, 1.0, 0.0).astype(BF16) for hh in range(HEADS_PER_GROUP)]
    low_half = lax.broadcasted_iota(jnp.int32, (Q, CHUNK), 1) < SSM_HEAD_DIM

    out_ssm = []
    for c in range(T // Q):
        rows = slice(c * Q, (c + 1) * Q)
        acum = jnp.dot(lower_ones, adt[rows], preferred_element_type=F32,
                       precision=lax.Precision.HIGHEST)
        acumT = jnp.dot(adtT[:, rows], upper_ones, preferred_element_type=F32,
                        precision=lax.Precision.HIGHEST)
        dtT_c = dtT[:, rows]
        wT = jnp.exp(acumT[:, Q - 1:Q] - acumT) * dtT_c
        total8 = jnp.broadcast_to(acum[Q - 1:Q, :], (8, CHUNK))
        chunk_decay = jnp.exp(jnp.dot(total8, expand, preferred_element_type=F32,
                                      precision=lax.Precision.HIGHEST))[0:1, :]
        xs_c = xs[rows]
        ys = []
        for g in range(SSM_GROUPS):
            gcols = slice(g * GROUP_W, (g + 1) * GROUP_W)
            b_g = bc[rows, g * SSM_STATE:(g + 1) * SSM_STATE]
            c_g = bc[rows, (SSM_GROUPS + g) * SSM_STATE:(SSM_GROUPS + g + 1) * SSM_STATE]
            cb = lax.dot_general(c_g.astype(BF16), b_g.astype(BF16), NT_DIMS, preferred_element_type=F32)
            b_gT = b_g.T
            xgb = xs_c[:, gcols].astype(BF16)
            s_prev = s_scr[g]
            x_bd = jnp.concatenate([xgb * head_mask[hh] for hh in range(HEADS_PER_GROUP)], axis=0)
            ms, bws, eacs = [], [], []
            for hh in range(HEADS_PER_GROUP):
                hidx = g * HEADS_PER_GROUP + hh
                acol = jnp.broadcast_to(acum[:, hidx:hidx + 1], (Q, Q))
                arow = acumT[hidx:hidx + 1, :]
                decay = jnp.exp(jnp.where(tril, acol - arow, MASK_VALUE))
                ms.append((cb * decay * dtT_c[hidx:hidx + 1, :]).astype(BF16))
                bws.append((b_gT * wT[hidx:hidx + 1, :]).astype(BF16))
                eacs.append(jnp.exp(acol))
            e_g = jnp.concatenate([jnp.where(low_half, eacs[0], eacs[1]),
                                   jnp.where(low_half, eacs[2], eacs[3])], axis=1)
            y_diag = jnp.dot(jnp.concatenate(ms, axis=1), x_bd, preferred_element_type=F32)
            y_off = jnp.dot(c_g.astype(BF16), s_prev.astype(BF16), preferred_element_type=F32) * e_g
            ys.append(y_diag + y_off)
            s_new = chunk_decay[:, gcols] * s_prev + jnp.dot(jnp.concatenate(bws, axis=1), x_bd,
                                                             preferred_element_type=F32)
            s_scr[g] = s_new
        y_c = jnp.concatenate(ys, axis=1) + xs_c * drep_ref[...]
        ug = y_c * _silu(z[rows])
        parts = []
        for g in range(SSM_GROUPS):
            blk = ug[:, g * GROUP_W:(g + 1) * GROUP_W]
            ms_g = jnp.mean(blk * blk, axis=-1, keepdims=True)
            parts.append(blk * lax.rsqrt(ms_g + NORM_EPS))
        yn = jnp.concatenate(parts, axis=1) * snw_ref[...]
        out_ssm.append(out_proj(yn, D_ATTN + D_CONV))

    out = out + jnp.concatenate(out_ssm, axis=0)
    if final_norm:
        out = _rms(out, fnw_ref[...])
    hout_ref[0] = out

    if emit_state:
        kout_ref[...] = kb
        for hd in range(ATTN_HEADS):
            vTout_ref[hd] = vTb[hd]
        tsc_out_ref[...] = ext_sc[0:TAIL, :]
        tx_out_ref[...] = ext_x[0:TAIL, :]
        s_out_ref[...] = s_scr[...]


def _const_spec(shape):
    nd = len(shape)
    return pl.BlockSpec(shape, lambda b, i, _nd=nd: (0,) * _nd, pipeline_mode=pl.Buffered(1))


def _layer_call(h, params, state, *, T, n_pad, has_prefix, emit_state, final_norm, lam_init):
    bsz, seq, _ = h.shape
    nt = seq // T
    kern = functools.partial(_layer_kernel, T=T, n_pad=n_pad, has_prefix=has_prefix,
                             emit_state=emit_state, final_norm=final_norm, lam_init=lam_init)
    ins = [h] + list(params) + list(state)
    in_specs = [pl.BlockSpec((1, T, D_MODEL), lambda b, i: (b, i, 0))]
    in_specs += [_const_spec(a.shape) for a in list(params) + list(state)]
    out_shape = [jax.ShapeDtypeStruct((bsz, seq, D_MODEL), F32)]
    out_specs = [pl.BlockSpec((1, T, D_MODEL), lambda b, i: (b, i, 0))]
    if emit_state:
        st_shapes = [((T, D_ATTN), BF16), ((ATTN_HEADS, V_ROWS, T), BF16), ((TAIL, D_CONV), F32),
                     ((TAIL, SSM_CONV_DIM), F32), ((SSM_GROUPS, SSM_STATE, GROUP_W), F32)]
        for shp, dt_ in st_shapes:
            out_shape.append(jax.ShapeDtypeStruct(shp, dt_))
            out_specs.append(pl.BlockSpec(shp, lambda b, i, _nd=len(shp): (0,) * _nd))
    scratch = [
        pltpu.VMEM((seq, D_ATTN), BF16),
        pltpu.VMEM((nt, ATTN_HEADS, V_ROWS, T), BF16),
        pltpu.VMEM((ATTN_HEADS, 2 * T, ATTN_V_DIM), BF16),
        pltpu.VMEM((ATTN_HEADS, V_ROWS, 2 * T), F32),
        pltpu.VMEM((TAIL + T, D_CONV), F32),
        pltpu.VMEM((TAIL + T, SSM_CONV_DIM), F32),
        pltpu.VMEM((SSM_GROUPS, SSM_STATE, GROUP_W), F32),
    ]
    return pl.pallas_call(
        kern,
        grid=(bsz, nt),
        in_specs=in_specs,
        out_specs=out_specs,
        out_shape=out_shape,
        scratch_shapes=scratch,
        compiler_params=pltpu.CompilerParams(
            dimension_semantics=("arbitrary", "arbitrary"),
            vmem_limit_bytes=VMEM_LIMIT_BYTES),
        name="layer_meta" if emit_state else "layer_main",
    )(*ins)


def kernel(x, meta_tokens, norm_w, w_in, w_out, attn_lambda, attn_norm_w, sconv_w, ssm_conv_w,
           ssm_conv_b, ssm_a_log, ssm_dt_bias, ssm_d, ssm_norm_w, final_norm_w):
    pad_heads = CHUNK - SSM_HEADS
    h_meta = jnp.concatenate([jnp.zeros((N_PAD, D_MODEL), F32), meta_tokens.astype(F32)], axis=0)[None]
    h_main = x.astype(F32)
    zero_state = (
        jnp.zeros((CHUNK, D_ATTN), BF16), jnp.zeros((ATTN_HEADS, V_ROWS, CHUNK), BF16),
        jnp.zeros((TAIL, D_CONV), F32), jnp.zeros((TAIL, SSM_CONV_DIM), F32),
        jnp.zeros((SSM_GROUPS, SSM_STATE, GROUP_W), F32),
    )
    fnw = final_norm_w.astype(F32)[None, :]
    for layer in range(DEPTH):
        lam_init = 0.8 - 0.6 * math.exp(-0.3 * layer)
        w_l = w_in[layer]
        w_dt = w_l[:, OFF_DT:OFF_DT + SSM_HEADS]
        params = (
            norm_w[layer].astype(F32)[None, :],
            w_l[:, :OFF_DT].astype(BF16),
            jnp.pad(w_dt, ((0, 0), (0, pad_heads))).astype(BF16),
            w_dt.T.astype(BF16),
            w_out[layer].astype(BF16),
            attn_lambda[layer].astype(F32),
            attn_norm_w[layer].astype(F32)[:, None],
            sconv_w[layer].astype(F32),
            ssm_conv_w[layer].astype(F32),
            ssm_conv_b[layer].astype(F32)[None, :],
            jnp.pad(ssm_a_log[layer].astype(F32), (0, pad_heads), constant_values=-jnp.inf)[None, :],
            ssm_a_log[layer].astype(F32)[:, None],
            jnp.pad(ssm_dt_bias[layer].astype(F32), (0, pad_heads))[None, :],
            ssm_dt_bias[layer].astype(F32)[:, None],
            jnp.repeat(ssm_d[layer].astype(F32), SSM_HEAD_DIM)[None, :],
            ssm_norm_w[layer].astype(F32)[None, :],
            fnw,
        )
        meta_out = _layer_call(h_meta, params, zero_state, T=CHUNK, n_pad=N_PAD, has_prefix=False,
                               emit_state=True, final_norm=False, lam_init=lam_init)
        h_meta, meta_state = meta_out[0], tuple(meta_out[1:])
        (h_main,) = _layer_call(h_main, params, meta_state, T=MAIN_TILE, n_pad=0, has_prefix=True,
                                emit_state=False, final_norm=(layer == DEPTH - 1), lam_init=lam_init)
    return h_main.astype(x.dtype)
```

```python
import functools
import math

import jax
import jax.numpy as jnp
from jax import lax
from jax.experimental import pallas as pl
from jax.experimental.pallas import tpu as pltpu

D_MODEL = 1024
DEPTH = 4
N_META = 16
D_ATTN = 512
D_CONV = 512
D_SSM = 1024
ATTN_HEADS = 4
ATTN_QK_DIM = 64
ATTN_V_DIM = 128
SCONV_WIDTH = 3
SSM_HEAD_DIM = 64
SSM_HEADS = 16
SSM_GROUPS = 4
SSM_STATE = 128
SSM_CONV_WIDTH = 4
SSM_CONV_DIM = 2048
CHUNK = 128
N_PAD = CHUNK - N_META
NORM_EPS = 1e-5
MASK_VALUE = -1e30
LOG2_E = 1.4426950408889634
V_ROWS = ATTN_V_DIM + 16
GROUP_W = D_SSM // SSM_GROUPS
HEADS_PER_GROUP = SSM_HEADS // SSM_GROUPS

OFF_QKV = 0
OFF_GATE_A = 3 * D_ATTN
OFF_SC = 4 * D_ATTN
OFF_Z = OFF_SC + 4 * D_CONV
OFF_XBC = OFF_Z + D_SSM
OFF_DT = OFF_XBC + SSM_CONV_DIM

TAIL = 8
MAIN_TILE = 256
VMEM_LIMIT_BYTES = 56 * 1024 * 1024

F32 = jnp.float32
BF16 = jnp.bfloat16
NT_DIMS = (((1,), (1,)), ((), ()))


def _silu(x):
    half = 0.5 * x
    return half * jnp.tanh(half) + half


def _softplus(x):
    return jnp.maximum(x, 0.0) + jnp.log1p(jnp.exp(-jnp.abs(x)))


def _rms(x, w):
    ms = jnp.mean(x * x, axis=-1, keepdims=True)
    return x * lax.rsqrt(ms + NORM_EPS) * w


def _attn_scores(q_scr, k_blk_fn):
    return [lax.dot_general(k_blk_fn(hd), q_scr[hd], NT_DIMS, preferred_element_type=F32)
            for hd in range(ATTN_HEADS)]


def _attn_softmax(s2s, maxes, mask, first, T):
    new, ps, alphas = [], [], []
    for hd in range(ATTN_HEADS):
        for mp in range(2):
            s = s2s[hd][:, mp * T:(mp + 1) * T]
            if mask is not None:
                s = jnp.where(mask, s, MASK_VALUE)
            m_new = jnp.max(s, axis=0, keepdims=True)
            if not first:
                m_old = maxes[2 * hd + mp]
                m_new = jnp.maximum(m_old, m_new)
                alphas.append(jnp.exp2(m_old - m_new))
            ps.append(jnp.exp2(s - m_new).astype(BF16))
            new.append(m_new)
    return tuple(new), ps, alphas


def _attn_accumulate(acc_scr, ps, alphas, vT_blk_fn, first):
    for hd in range(ATTN_HEADS):
        pv = jnp.dot(vT_blk_fn(hd), jnp.concatenate(ps[2 * hd:2 * hd + 2], axis=1),
                     preferred_element_type=F32)
        if first:
            acc_scr[hd] = pv
        else:
            acc_scr[hd] = jnp.concatenate(alphas[2 * hd:2 * hd + 2], axis=1) * acc_scr[hd] + pv


def _layer_kernel(h_ref, nw_ref, win_ref, wdt_ref, wdtT_ref, wout_ref, lam_ref, anw_ref,
                  scw_ref, xw_ref, xb_ref, alog_row_ref, alog_col_ref, dtb_row_ref, dtb_col_ref,
                  drep_ref, snw_ref, fnw_ref, kpre_ref, vpreT_ref, tsc_ref, tx_ref, s0_ref,
                  *rest, T, n_pad, has_prefix, emit_state, final_norm, lam_init):
    if emit_state:
        (hout_ref, kout_ref, vTout_ref, tsc_out_ref, tx_out_ref, s_out_ref,
         k_scr, vT_scr, q_scr, acc_scr, s2_scr, ext_sc, ext_x, s_scr) = rest
    else:
        hout_ref, k_scr, vT_scr, q_scr, acc_scr, s2_scr, ext_sc, ext_x, s_scr = rest

    i = pl.program_id(1)
    Q = CHUNK

    @pl.when(i == 0)
    def _init():
        ext_sc[0:TAIL, :] = tsc_ref[...]
        ext_x[0:TAIL, :] = tx_ref[...]
        s_scr[...] = s0_ref[...]

    if n_pad:
        row_ok = lax.broadcasted_iota(jnp.int32, (T, 1), 0) >= n_pad
        col_ok = lax.broadcasted_iota(jnp.int32, (1, T), 1) >= n_pad

    h = h_ref[0]
    u = _rms(h, nw_ref[...])
    if n_pad:
        u = jnp.where(row_ok, u, 0.0)
    ub = u.astype(BF16)

    def proj(off, width):
        return jnp.dot(ub, win_ref[:, off:off + width], preferred_element_type=F32)

    def out_proj(y, off):
        return jnp.dot(y.astype(BF16), wout_ref[off:off + y.shape[1], :], preferred_element_type=F32)

    def causal_conv(raw, ext_ref, cols, w_ref, width, bias):
        ext_ref[TAIL:TAIL + T, cols] = raw
        acc = raw * w_ref[width - 1:width, cols]
        if bias is not None:
            acc = acc + bias
        for j in range(width - 1):
            shift = width - 1 - j
            acc = acc + ext_ref[TAIL - shift:TAIL - shift + T, cols] * w_ref[j:j + 1, cols]
        ext_ref[0:TAIL, cols] = ext_ref[T:T + TAIL, cols]
        return acc

    def head_cols(hd):
        return slice(hd * ATTN_V_DIM, (hd + 1) * ATTN_V_DIM)

    qkv = proj(OFF_QKV, 3 * D_ATTN)
    sc = proj(OFF_SC, 4 * D_CONV)
    kb = qkv[:, D_ATTN:2 * D_ATTN].astype(BF16)
    vT = qkv[:, 2 * D_ATTN:3 * D_ATTN].T
    ones_rows = (lax.broadcasted_iota(jnp.int32, (V_ROWS - ATTN_V_DIM, T), 0) == 0).astype(BF16)
    vTb = [jnp.concatenate([vT[hd * ATTN_V_DIM:(hd + 1) * ATTN_V_DIM].astype(BF16), ones_rows], axis=0)
           for hd in range(ATTN_HEADS)]
    k_scr[pl.ds(pl.multiple_of(i * T, CHUNK), T), :] = kb
    for hd in range(ATTN_HEADS):
        vT_scr[i, hd] = vTb[hd]

    q = qkv[:, 0:D_ATTN] * (ATTN_QK_DIM ** -0.5 * LOG2_E)
    lane = lax.broadcasted_iota(jnp.int32, (T, ATTN_V_DIM), 1)
    for hd in range(ATTN_HEADS):
        qh = q[:, head_cols(hd)]
        q_scr[hd, 0:T, :] = jnp.where(lane < ATTN_QK_DIM, qh, 0.0).astype(BF16)
        q_scr[hd, T:2 * T, :] = jnp.where(lane >= ATTN_QK_DIM, qh, 0.0).astype(BF16)

    cch = sc[:, D_CONV:2 * D_CONV] * sc[:, 2 * D_CONV:3 * D_CONV]
    conv = causal_conv(cch, ext_sc, slice(0, D_CONV), scw_ref, SCONV_WIDTH, None)
    y_conv = sc[:, 0:D_CONV] * conv * _silu(sc[:, 3 * D_CONV:4 * D_CONV])

    def tile_keys(j):
        off = pl.multiple_of(j * T, CHUNK)
        return lambda hd: k_scr[pl.ds(off, T), head_cols(hd)]

    def stash_scores(s2s):
        for hd in range(ATTN_HEADS):
            s2_scr[hd] = s2s[hd]

    maxes = None
    if has_prefix:
        prefix_mask = lax.broadcasted_iota(jnp.int32, (CHUNK, T), 0) >= N_PAD
        s2p = _attn_scores(q_scr, lambda hd: kpre_ref[:, head_cols(hd)])
        maxes, ps, alphas = _attn_softmax(s2p, None, prefix_mask, True, T)
        s2n = _attn_scores(q_scr, tile_keys(0))
        _attn_accumulate(acc_scr, ps, alphas, lambda hd: vpreT_ref[hd], True)
        stash_scores(s2n)

        def body(j, mx):
            new, ps, alphas = _attn_softmax([s2_scr[hd] for hd in range(ATTN_HEADS)], mx, None, False, T)
            s2n = _attn_scores(q_scr, tile_keys(j + 1))
            _attn_accumulate(acc_scr, ps, alphas, lambda hd: vT_scr[j, hd], False)
            stash_scores(s2n)
            return new

        maxes = lax.fori_loop(0, i, body, maxes)
        s2d = [s2_scr[hd] for hd in range(ATTN_HEADS)]
    else:
        s2d = _attn_scores(q_scr, lambda hd: kb[:, head_cols(hd)])

    key_i = lax.broadcasted_iota(jnp.int32, (T, T), 0)
    qry_i = lax.broadcasted_iota(jnp.int32, (T, T), 1)
    diag_mask = key_i <= qry_i
    if n_pad:
        diag_mask = jnp.logical_and(diag_mask, key_i >= n_pad)
    _, ps, alphas = _attn_softmax(s2d, maxes, diag_mask, not has_prefix, T)
    xs_raw = proj(OFF_XBC, D_SSM)
    _attn_accumulate(acc_scr, ps, alphas, lambda hd: vTb[hd], not has_prefix)
    bc_raw = proj(OFF_XBC + D_SSM, SSM_CONV_DIM - D_SSM)
    xs = _silu(causal_conv(xs_raw, ext_x, slice(0, D_SSM), xw_ref, SSM_CONV_WIDTH, xb_ref[:, 0:D_SSM]))
    gate_a = proj(OFF_GATE_A, D_ATTN)
    z = proj(OFF_Z, D_SSM)
    bc = _silu(causal_conv(bc_raw, ext_x, slice(D_SSM, SSM_CONV_DIM), xw_ref, SSM_CONV_WIDTH,
                           xb_ref[:, D_SSM:SSM_CONV_DIM]))
    if n_pad:
        xs = jnp.where(row_ok, xs, 0.0)
        bc = jnp.where(row_ok, bc, 0.0)

    lp = lam_ref[...]
    lam = (jnp.exp(jnp.sum(lp[0:1] * lp[1:2], axis=-1, keepdims=True))
           - jnp.exp(jnp.sum(lp[2:3] * lp[3:4], axis=-1, keepdims=True)) + lam_init)
    y_attn = []
    for hd in range(ATTN_HEADS):
        dv = ATTN_V_DIM
        inv_l0 = 1.0 / acc_scr[hd, dv:dv + 1, 0:T]
        inv_l1 = 1.0 / acc_scr[hd, dv:dv + 1, T:2 * T]
        oT = acc_scr[hd, 0:dv, 0:T] * inv_l0 - lam * (acc_scr[hd, 0:dv, T:2 * T] * inv_l1)
        ms = jnp.mean(oT * oT, axis=0, keepdims=True)
        oT = oT * lax.rsqrt(ms + NORM_EPS) * anw_ref[...] * (1.0 - lam_init)
        y_attn.append(oT.T * _silu(gate_a[:, head_cols(hd)]))
    out = h + out_proj(y_conv, D_ATTN)
    out = out + out_proj(jnp.concatenate(y_attn, axis=1), 0)

    dt = _softplus(jnp.dot(ub, wdt_ref[...], preferred_element_type=F32) + dtb_row_ref[...])
    dtT = _softplus(lax.dot_general(wdtT_ref[...], ub, NT_DIMS, preferred_element_type=F32)
                    + dtb_col_ref[...])
    if n_pad:
        dt = jnp.where(row_ok, dt, 0.0)
        dtT = jnp.where(col_ok, dtT, 0.0)
    adt = dt * (-jnp.exp(alog_row_ref[...]))
    adtT = dtT * (-jnp.exp(alog_col_ref[...]))

    qr = lax.broadcasted_iota(jnp.int32, (Q, Q), 0)
    qc = lax.broadcasted_iota(jnp.int32, (Q, Q), 1)
    tril = qr >= qc
    lower_ones = tril.astype(F32)
    upper_ones = (qr <= qc).astype(F32)
    expand = (lax.broadcasted_iota(jnp.int32, (CHUNK, D_SSM), 0)
              == (lax.broadcasted_iota(jnp.int32, (CHUNK, D_SSM), 1) // SSM_HEAD_DIM)).astype(F32)
    colblk = lax.broadcasted_iota(jnp.int32, (Q, GROUP_W), 1) // SSM_HEAD_DIM
    head_mask = [jnp.where(colblk == hh, 1.0, 0.0).astype(BF16) for hh in range(HEADS_PER_GROUP)]
    low_half = lax.broadcasted_iota(jnp.int32, (Q, CHUNK), 1) < SSM_HEAD_DIM

    out_ssm = []
    for c in range(T // Q):
        rows = slice(c * Q, (c + 1) * Q)
        acum = jnp.dot(lower_ones, adt[rows], preferred_element_type=F32,
                       precision=lax.Precision.HIGHEST)
        acumT = jnp.dot(adtT[:, rows], upper_ones, preferred_element_type=F32,
                        precision=lax.Precision.HIGHEST)
        dtT_c = dtT[:, rows]
        wT = jnp.exp(acumT[:, Q - 1:Q] - acumT) * dtT_c
        total8 = jnp.broadcast_to(acum[Q - 1:Q, :], (8, CHUNK))
        chunk_decay = jnp.exp(jnp.dot(total8, expand, preferred_element_type=F32,
                                      precision=lax.Precision.HIGHEST))[0:1, :]
        xs_c = xs[rows]
        b_gs = [bc[rows, g * SSM_STATE:(g + 1) * SSM_STATE] for g in range(SSM_GROUPS)]
        c_gs = [bc[rows, (SSM_GROUPS + g) * SSM_STATE:(SSM_GROUPS + g + 1) * SSM_STATE].astype(BF16)
                for g in range(SSM_GROUPS)]
        s_prevs = [s_scr[g] for g in range(SSM_GROUPS)]
        cbs = [lax.dot_general(c_gs[g], b_gs[g].astype(BF16), NT_DIMS, preferred_element_type=F32)
               for g in range(SSM_GROUPS)]
        y_offs = [jnp.dot(c_gs[g], s_prevs[g].astype(BF16), preferred_element_type=F32)
                  for g in range(SSM_GROUPS)]
        lhs_diag, lhs_state, x_bds, e_gs = [], [], [], []
        for g in range(SSM_GROUPS):
            gcols = slice(g * GROUP_W, (g + 1) * GROUP_W)
            b_gT = b_gs[g].T
            xgb = xs_c[:, gcols].astype(BF16)
            x_bds.append(jnp.concatenate([xgb * head_mask[hh] for hh in range(HEADS_PER_GROUP)], axis=0))
            ms, bws, eacs = [], [], []
            for hh in range(HEADS_PER_GROUP):
                hidx = g * HEADS_PER_GROUP + hh
                acol = jnp.broadcast_to(acum[:, hidx:hidx + 1], (Q, Q))
                arow = acumT[hidx:hidx + 1, :]
                decay = jnp.exp(jnp.where(tril, acol - arow, MASK_VALUE))
                ms.append((cbs[g] * decay * dtT_c[hidx:hidx + 1, :]).astype(BF16))
                bws.append((b_gT * wT[hidx:hidx + 1, :]).astype(BF16))
                eacs.append(jnp.exp(acol))
            e_gs.append(jnp.concatenate([jnp.where(low_half, eacs[0], eacs[1]),
                                         jnp.where(low_half, eacs[2], eacs[3])], axis=1))
            lhs_diag.append(jnp.concatenate(ms, axis=1))
            lhs_state.append(jnp.concatenate(bws, axis=1))
        y_diags = [jnp.dot(lhs_diag[g], x_bds[g], preferred_element_type=F32) for g in range(SSM_GROUPS)]
        s_incs = [jnp.dot(lhs_state[g], x_bds[g], preferred_element_type=F32) for g in range(SSM_GROUPS)]
        ys = []
        for g in range(SSM_GROUPS):
            gcols = slice(g * GROUP_W, (g + 1) * GROUP_W)
            ys.append(y_diags[g] + y_offs[g] * e_gs[g])
            s_scr[g] = chunk_decay[:, gcols] * s_prevs[g] + s_incs[g]
        y_c = jnp.concatenate(ys, axis=1) + xs_c * drep_ref[...]
        ug = y_c * _silu(z[rows])
        parts = []
        for g in range(SSM_GROUPS):
            blk = ug[:, g * GROUP_W:(g + 1) * GROUP_W]
            ms_g = jnp.mean(blk * blk, axis=-1, keepdims=True)
            parts.append(blk * lax.rsqrt(ms_g + NORM_EPS))
        yn = jnp.concatenate(parts, axis=1) * snw_ref[...]
        out_ssm.append(out_proj(yn, D_ATTN + D_CONV))

    out = out + jnp.concatenate(out_ssm, axis=0)
    if final_norm:
        out = _rms(out, fnw_ref[...])
    hout_ref[0] = out

    if emit_state:
        kout_ref[...] = kb
        for hd in range(ATTN_HEADS):
            vTout_ref[hd] = vTb[hd]
        tsc_out_ref[...] = ext_sc[0:TAIL, :]
        tx_out_ref[...] = ext_x[0:TAIL, :]
        s_out_ref[...] = s_scr[...]


def _const_spec(shape):
    nd = len(shape)
    return pl.BlockSpec(shape, lambda b, i, _nd=nd: (0,) * _nd, pipeline_mode=pl.Buffered(1))


def _layer_call(h, params, state, *, T, n_pad, has_prefix, emit_state, final_norm, lam_init):
    bsz, seq, _ = h.shape
    nt = seq // T
    kern = functools.partial(_layer_kernel, T=T, n_pad=n_pad, has_prefix=has_prefix,
                             emit_state=emit_state, final_norm=final_norm, lam_init=lam_init)
    ins = [h] + list(params) + list(state)
    in_specs = [pl.BlockSpec((1, T, D_MODEL), lambda b, i: (b, i, 0))]
    in_specs += [_const_spec(a.shape) for a in list(params) + list(state)]
    out_shape = [jax.ShapeDtypeStruct((bsz, seq, D_MODEL), F32)]
    out_specs = [pl.BlockSpec((1, T, D_MODEL), lambda b, i: (b, i, 0))]
    if emit_state:
        st_shapes = [((T, D_ATTN), BF16), ((ATTN_HEADS, V_ROWS, T), BF16), ((TAIL, D_CONV), F32),
                     ((TAIL, SSM_CONV_DIM), F32), ((SSM_GROUPS, SSM_STATE, GROUP_W), F32)]
        for shp, dt_ in st_shapes:
            out_shape.append(jax.ShapeDtypeStruct(shp, dt_))
            out_specs.append(pl.BlockSpec(shp, lambda b, i, _nd=len(shp): (0,) * _nd))
    scratch = [
        pltpu.VMEM((seq, D_ATTN), BF16),
        pltpu.VMEM((nt, ATTN_HEADS, V_ROWS, T), BF16),
        pltpu.VMEM((ATTN_HEADS, 2 * T, ATTN_V_DIM), BF16),
        pltpu.VMEM((ATTN_HEADS, V_ROWS, 2 * T), F32),
        pltpu.VMEM((ATTN_HEADS, T, 2 * T), F32),
        pltpu.VMEM((TAIL + T, D_CONV), F32),
        pltpu.VMEM((TAIL + T, SSM_CONV_DIM), F32),
        pltpu.VMEM((SSM_GROUPS, SSM_STATE, GROUP_W), F32),
    ]
    return pl.pallas_call(
        kern,
        grid=(bsz, nt),
        in_specs=in_specs,
        out_specs=out_specs,
        out_shape=out_shape,
        scratch_shapes=scratch,
        compiler_params=pltpu.CompilerParams(
            dimension_semantics=("arbitrary", "arbitrary"),
            vmem_limit_bytes=VMEM_LIMIT_BYTES),
        name="layer_meta" if emit_state else "layer_main",
    )(*ins)


def kernel(x, meta_tokens, norm_w, w_in, w_out, attn_lambda, attn_norm_w, sconv_w, ssm_conv_w,
           ssm_conv_b, ssm_a_log, ssm_dt_bias, ssm_d, ssm_norm_w, final_norm_w):
    pad_heads = CHUNK - SSM_HEADS
    h_meta = jnp.concatenate([jnp.zeros((N_PAD, D_MODEL), F32), meta_tokens.astype(F32)], axis=0)[None]
    h_main = x.astype(F32)
    zero_state = (
        jnp.zeros((CHUNK, D_ATTN), BF16), jnp.zeros((ATTN_HEADS, V_ROWS, CHUNK), BF16),
        jnp.zeros((TAIL, D_CONV), F32), jnp.zeros((TAIL, SSM_CONV_DIM), F32),
        jnp.zeros((SSM_GROUPS, SSM_STATE, GROUP_W), F32),
    )
    fnw = final_norm_w.astype(F32)[None, :]
    for layer in range(DEPTH):
        lam_init = 0.8 - 0.6 * math.exp(-0.3 * layer)
        w_l = w_in[layer]
        w_dt = w_l[:, OFF_DT:OFF_DT + SSM_HEADS]
        params = (
            norm_w[layer].astype(F32)[None, :],
            w_l[:, :OFF_DT].astype(BF16),
            jnp.pad(w_dt, ((0, 0), (0, pad_heads))).astype(BF16),
            w_dt.T.astype(BF16),
            w_out[layer].astype(BF16),
            attn_lambda[layer].astype(F32),
            attn_norm_w[layer].astype(F32)[:, None],
            sconv_w[layer].astype(F32),
            ssm_conv_w[layer].astype(F32),
            ssm_conv_b[layer].astype(F32)[None, :],
            jnp.pad(ssm_a_log[layer].astype(F32), (0, pad_heads), constant_values=-jnp.inf)[None, :],
            ssm_a_log[layer].astype(F32)[:, None],
            jnp.pad(ssm_dt_bias[layer].astype(F32), (0, pad_heads))[None, :],
            ssm_dt_bias[layer].astype(F32)[:, None],
            jnp.repeat(ssm_d[layer].astype(F32), SSM_HEAD_DIM)[None, :],
            ssm_norm_w[layer].astype(F32)[None, :],
            fnw,
        )
        meta_out = _layer_call(h_meta, params, zero_state, T=CHUNK, n_pad=N_PAD, has_prefix=False,
                               emit_state=True, final_norm=False, lam_init=lam_init)
        h_meta, meta_state = meta_out[0], tuple(meta_out[1:])
        (h_main,) = _layer_call(h_main, params, meta_state, T=MAIN_TILE, n_pad=0, has_prefix=True,
                                emit_state=False, final_norm=(layer == DEPTH - 1), lam_init=lam_init)
    return h_main.astype(x.dtype)
```

```python
import functools
import math

import jax
import jax.numpy as jnp
from jax import lax
from jax.experimental import pallas as pl
from jax.experimental.pallas import tpu as pltpu

D_MODEL = 1024
DEPTH = 4
N_META = 16
D_ATTN = 512
D_CONV = 512
D_SSM = 1024
ATTN_HEADS = 4
ATTN_QK_DIM = 64
ATTN_V_DIM = 128
SCONV_WIDTH = 3
SSM_HEAD_DIM = 64
SSM_HEADS = 16
SSM_GROUPS = 4
SSM_STATE = 128
SSM_CONV_WIDTH = 4
SSM_CONV_DIM = 2048
CHUNK = 128
N_PAD = CHUNK - N_META
NORM_EPS = 1e-5
MASK_VALUE = -1e30
LOG2_E = 1.4426950408889634
V_ROWS = ATTN_V_DIM + 16
GROUP_W = D_SSM // SSM_GROUPS
HEADS_PER_GROUP = SSM_HEADS // SSM_GROUPS

OFF_QKV = 0
OFF_GATE_A = 3 * D_ATTN
OFF_SC = 4 * D_ATTN
OFF_Z = OFF_SC + 4 * D_CONV
OFF_XBC = OFF_Z + D_SSM
OFF_DT = OFF_XBC + SSM_CONV_DIM

TAIL = 8
MAIN_TILE = 256
VMEM_LIMIT_BYTES = 56 * 1024 * 1024

F32 = jnp.float32
BF16 = jnp.bfloat16
NT_DIMS = (((1,), (1,)), ((), ()))


def _silu(x):
    half = 0.5 * x
    return half * jnp.tanh(half) + half


def _softplus(x):
    return jnp.maximum(x, 0.0) + jnp.log1p(jnp.exp(-jnp.abs(x)))


def _rms(x, w):
    ms = jnp.mean(x * x, axis=-1, keepdims=True)
    return x * lax.rsqrt(ms + NORM_EPS) * w


def _attn_scores(q_scr, k_blk_fn):
    return [lax.dot_general(k_blk_fn(hd), q_scr[hd], NT_DIMS, preferred_element_type=F32)
            for hd in range(ATTN_HEADS)]


def _attn_softmax(s2s, maxes, mask, first, T):
    new, ps, alphas = [], [], []
    for hd in range(ATTN_HEADS):
        for mp in range(2):
            s = s2s[hd][:, mp * T:(mp + 1) * T]
            if mask is not None:
                s = jnp.where(mask, s, MASK_VALUE)
            m_new = jnp.max(s, axis=0, keepdims=True)
            if not first:
                m_old = maxes[2 * hd + mp]
                m_new = jnp.maximum(m_old, m_new)
                alphas.append(jnp.exp2(m_old - m_new))
            ps.append(jnp.exp2(s - m_new).astype(BF16))
            new.append(m_new)
    return tuple(new), ps, alphas


def _attn_accumulate(acc_scr, ps, alphas, vT_blk_fn, first):
    for hd in range(ATTN_HEADS):
        pv = jnp.dot(vT_blk_fn(hd), jnp.concatenate(ps[2 * hd:2 * hd + 2], axis=1),
                     preferred_element_type=F32)
        if first:
            acc_scr[hd] = pv
        else:
            acc_scr[hd] = jnp.concatenate(alphas[2 * hd:2 * hd + 2], axis=1) * acc_scr[hd] + pv


def _layer_kernel(h_ref, nw_ref, win_ref, wdt_ref, wdtT_ref, wout_ref, lam_ref, anw_ref,
                  scw_ref, xw_ref, xb_ref, alog_row_ref, alog_col_ref, dtb_row_ref, dtb_col_ref,
                  drep_ref, snw_ref, fnw_ref, kpre_ref, vpreT_ref, tsc_ref, tx_ref, s0_ref,
                  *rest, T, n_pad, has_prefix, emit_state, final_norm, lam_init):
    if emit_state:
        (hout_ref, kout_ref, vTout_ref, tsc_out_ref, tx_out_ref, s_out_ref,
         k_scr, vT_scr, q_scr, acc_scr, s2_scr, ext_sc, ext_x, s_scr) = rest
    else:
        hout_ref, k_scr, vT_scr, q_scr, acc_scr, s2_scr, ext_sc, ext_x, s_scr = rest

    i = pl.program_id(1)
    Q = CHUNK

    @pl.when(i == 0)
    def _init():
        ext_sc[0:TAIL, :] = tsc_ref[...]
        ext_x[0:TAIL, :] = tx_ref[...]
        s_scr[...] = s0_ref[...]

    if n_pad:
        row_ok = lax.broadcasted_iota(jnp.int32, (T, 1), 0) >= n_pad
        col_ok = lax.broadcasted_iota(jnp.int32, (1, T), 1) >= n_pad

    h = h_ref[0]
    u = _rms(h, nw_ref[...])
    if n_pad:
        u = jnp.where(row_ok, u, 0.0)
    ub = u.astype(BF16)

    def proj(off, width):
        return jnp.dot(ub, win_ref[:, off:off + width], preferred_element_type=F32)

    def out_proj(y, off):
        return jnp.dot(y.astype(BF16), wout_ref[off:off + y.shape[1], :], preferred_element_type=F32)

    def causal_conv(raw, ext_ref, cols, w_ref, width, bias):
        ext_ref[TAIL:TAIL + T, cols] = raw
        acc = raw * w_ref[width - 1:width, cols]
        if bias is not None:
            acc = acc + bias
        for j in range(width - 1):
            shift = width - 1 - j
            acc = acc + ext_ref[TAIL - shift:TAIL - shift + T, cols] * w_ref[j:j + 1, cols]
        ext_ref[0:TAIL, cols] = ext_ref[T:T + TAIL, cols]
        return acc

    def head_cols(hd):
        return slice(hd * ATTN_V_DIM, (hd + 1) * ATTN_V_DIM)

    qkv = proj(OFF_QKV, 3 * D_ATTN)
    sc = proj(OFF_SC, 4 * D_CONV)
    kb = qkv[:, D_ATTN:2 * D_ATTN].astype(BF16)
    vT = qkv[:, 2 * D_ATTN:3 * D_ATTN].T
    ones_rows = (lax.broadcasted_iota(jnp.int32, (V_ROWS - ATTN_V_DIM, T), 0) == 0).astype(BF16)
    vTb = [jnp.concatenate([vT[hd * ATTN_V_DIM:(hd + 1) * ATTN_V_DIM].astype(BF16), ones_rows], axis=0)
           for hd in range(ATTN_HEADS)]
    k_scr[pl.ds(pl.multiple_of(i * T, CHUNK), T), :] = kb
    for hd in range(ATTN_HEADS):
        vT_scr[i, hd] = vTb[hd]

    q = qkv[:, 0:D_ATTN] * (ATTN_QK_DIM ** -0.5 * LOG2_E)
    lane = lax.broadcasted_iota(jnp.int32, (T, ATTN_V_DIM), 1)
    for hd in range(ATTN_HEADS):
        qh = q[:, head_cols(hd)]
        q_scr[hd, 0:T, :] = jnp.where(lane < ATTN_QK_DIM, qh, 0.0).astype(BF16)
        q_scr[hd, T:2 * T, :] = jnp.where(lane >= ATTN_QK_DIM, qh, 0.0).astype(BF16)

    cch = sc[:, D_CONV:2 * D_CONV] * sc[:, 2 * D_CONV:3 * D_CONV]
    conv = causal_conv(cch, ext_sc, slice(0, D_CONV), scw_ref, SCONV_WIDTH, None)
    y_conv = sc[:, 0:D_CONV] * conv * _silu(sc[:, 3 * D_CONV:4 * D_CONV])

    def tile_keys(j):
        off = pl.multiple_of(j * T, CHUNK)
        return lambda hd: k_scr[pl.ds(off, T), head_cols(hd)]

    def stash_scores(s2s):
        for hd in range(ATTN_HEADS):
            s2_scr[hd] = s2s[hd]

    maxes = None
    if has_prefix:
        prefix_mask = lax.broadcasted_iota(jnp.int32, (CHUNK, T), 0) >= N_PAD
        s2p = _attn_scores(q_scr, lambda hd: kpre_ref[:, head_cols(hd)])
        maxes, ps, alphas = _attn_softmax(s2p, None, prefix_mask, True, T)
        s2n = _attn_scores(q_scr, tile_keys(0))
        _attn_accumulate(acc_scr, ps, alphas, lambda hd: vpreT_ref[hd], True)
        stash_scores(s2n)

        def body(j, mx):
            new, ps, alphas = _attn_softmax([s2_scr[hd] for hd in range(ATTN_HEADS)], mx, None, False, T)
            s2n = _attn_scores(q_scr, tile_keys(j + 1))
            _attn_accumulate(acc_scr, ps, alphas, lambda hd: vT_scr[j, hd], False)
            stash_scores(s2n)
            return new

        maxes = lax.fori_loop(0, i, body, maxes)
        s2d = [s2_scr[hd] for hd in range(ATTN_HEADS)]
    else:
        s2d = _attn_scores(q_scr, lambda hd: kb[:, head_cols(hd)])

    key_i = lax.broadcasted_iota(jnp.int32, (T, T), 0)
    qry_i = lax.broadcasted_iota(jnp.int32, (T, T), 1)
    diag_mask = key_i <= qry_i
    if n_pad:
        diag_mask = jnp.logical_and(diag_mask, key_i >= n_pad)
    _, ps, alphas = _attn_softmax(s2d, maxes, diag_mask, not has_prefix, T)
    xs_raw = proj(OFF_XBC, D_SSM)
    _attn_accumulate(acc_scr, ps, alphas, lambda hd: vTb[hd], not has_prefix)
    bc_raw = proj(OFF_XBC + D_SSM, SSM_CONV_DIM - D_SSM)
    xs = _silu(causal_conv(xs_raw, ext_x, slice(0, D_SSM), xw_ref, SSM_CONV_WIDTH, xb_ref[:, 0:D_SSM]))
    gate_a = proj(OFF_GATE_A, D_ATTN)
    z = proj(OFF_Z, D_SSM)
    bc = _silu(causal_conv(bc_raw, ext_x, slice(D_SSM, SSM_CONV_DIM), xw_ref, SSM_CONV_WIDTH,
                           xb_ref[:, D_SSM:SSM_CONV_DIM]))
    if n_pad:
        xs = jnp.where(row_ok, xs, 0.0)
        bc = jnp.where(row_ok, bc, 0.0)

    lp = lam_ref[...]
    lam = (jnp.exp(jnp.sum(lp[0:1] * lp[1:2], axis=-1, keepdims=True))
           - jnp.exp(jnp.sum(lp[2:3] * lp[3:4], axis=-1, keepdims=True)) + lam_init)
    y_attn = []
    for hd in range(ATTN_HEADS):
        dv = ATTN_V_DIM
        inv_l0 = 1.0 / acc_scr[hd, dv:dv + 1, 0:T]
        inv_l1 = 1.0 / acc_scr[hd, dv:dv + 1, T:2 * T]
        oT = acc_scr[hd, 0:dv, 0:T] * inv_l0 - lam * (acc_scr[hd, 0:dv, T:2 * T] * inv_l1)
        ms = jnp.mean(oT * oT, axis=0, keepdims=True)
        oT = oT * lax.rsqrt(ms + NORM_EPS) * anw_ref[...] * (1.0 - lam_init)
        y_attn.append(oT.T * _silu(gate_a[:, head_cols(hd)]))
    out = h + out_proj(y_conv, D_ATTN)
    out = out + out_proj(jnp.concatenate(y_attn, axis=1), 0)

    dt = _softplus(jnp.dot(ub, wdt_ref[...], preferred_element_type=F32) + dtb_row_ref[...])
    dtT = _softplus(lax.dot_general(wdtT_ref[...], ub, NT_DIMS, preferred_element_type=F32)
                    + dtb_col_ref[...])
    if n_pad:
        dt = jnp.where(row_ok, dt, 0.0)
        dtT = jnp.where(col_ok, dtT, 0.0)
    adt = dt * (-LOG2_E * jnp.exp(alog_row_ref[...]))
    adtT = dtT * (-LOG2_E * jnp.exp(alog_col_ref[...]))

    qr = lax.broadcasted_iota(jnp.int32, (Q, Q), 0)
    qc = lax.broadcasted_iota(jnp.int32, (Q, Q), 1)
    tril = qr >= qc
    lower_ones = tril.astype(F32)
    upper_ones = (qr <= qc).astype(F32)
    expand = (lax.broadcasted_iota(jnp.int32, (CHUNK, D_SSM), 0)
              == (lax.broadcasted_iota(jnp.int32, (CHUNK, D_SSM), 1) // SSM_HEAD_DIM)).astype(F32)
    colblk = lax.broadcasted_iota(jnp.int32, (Q, GROUP_W), 1) // SSM_HEAD_DIM
    head_mask = [jnp.where(colblk == hh, 1.0, 0.0).astype(BF16) for hh in range(HEADS_PER_GROUP)]
    low_half = lax.broadcasted_iota(jnp.int32, (Q, CHUNK), 1) < SSM_HEAD_DIM

    out_ssm = []
    for c in range(T // Q):
        rows = slice(c * Q, (c + 1) * Q)
        acum = jnp.dot(lower_ones, adt[rows], preferred_element_type=F32,
                       precision=lax.Precision.HIGHEST)
        acumT = jnp.dot(adtT[:, rows], upper_ones, preferred_element_type=F32,
                        precision=lax.Precision.HIGHEST)
        dtT_c = dtT[:, rows]
        wT = jnp.exp2(acumT[:, Q - 1:Q] - acumT) * dtT_c
        total8 = jnp.broadcast_to(acum[Q - 1:Q, :], (8, CHUNK))
        chunk_decay = jnp.exp2(jnp.dot(total8, expand, preferred_element_type=F32,
                                       precision=lax.Precision.HIGHEST))[0:1, :]
        xs_c = xs[rows]
        b_gs = [bc[rows, g * SSM_STATE:(g + 1) * SSM_STATE] for g in range(SSM_GROUPS)]
        c_gs = [bc[rows, (SSM_GROUPS + g) * SSM_STATE:(SSM_GROUPS + g + 1) * SSM_STATE].astype(BF16)
                for g in range(SSM_GROUPS)]
        s_prevs = [s_scr[g] for g in range(SSM_GROUPS)]
        cbs = [lax.dot_general(c_gs[g], b_gs[g].astype(BF16), NT_DIMS, preferred_element_type=F32)
               for g in range(SSM_GROUPS)]
        y_offs = [jnp.dot(c_gs[g], s_prevs[g].astype(BF16), preferred_element_type=F32)
                  for g in range(SSM_GROUPS)]
        lhs_diag, lhs_state, x_bds, e_gs = [], [], [], []
        for g in range(SSM_GROUPS):
            gcols = slice(g * GROUP_W, (g + 1) * GROUP_W)
            b_gT = b_gs[g].T
            xgb = xs_c[:, gcols].astype(BF16)
            x_bds.append(jnp.concatenate([xgb * head_mask[hh] for hh in range(HEADS_PER_GROUP)], axis=0))
            ms, bws, eacs = [], [], []
            for hh in range(HEADS_PER_GROUP):
                hidx = g * HEADS_PER_GROUP + hh
                acol = jnp.broadcast_to(acum[:, hidx:hidx + 1], (Q, Q))
                arow = acumT[hidx:hidx + 1, :]
                decay = jnp.exp2(jnp.where(tril, acol - arow, MASK_VALUE))
                ms.append((cbs[g] * decay * dtT_c[hidx:hidx + 1, :]).astype(BF16))
                bws.append((b_gT * wT[hidx:hidx + 1, :]).astype(BF16))
                eacs.append(jnp.exp2(acol))
            e_gs.append(jnp.concatenate([jnp.where(low_half, eacs[0], eacs[1]),
                                         jnp.where(low_half, eacs[2], eacs[3])], axis=1))
            lhs_diag.append(jnp.concatenate(ms, axis=1))
            lhs_state.append(jnp.concatenate(bws, axis=1))
        y_diags = [jnp.dot(lhs_diag[g], x_bds[g], preferred_element_type=F32) for g in range(SSM_GROUPS)]
        s_incs = [jnp.dot(lhs_state[g], x_bds[g], preferred_element_type=F32) for g in range(SSM_GROUPS)]
        ys = []
        for g in range(SSM_GROUPS):
            gcols = slice(g * GROUP_W, (g + 1) * GROUP_W)
            ys.append(y_diags[g] + y_offs[g] * e_gs[g])
            s_scr[g] = chunk_decay[:, gcols] * s_prevs[g] + s_incs[g]
        y_c = jnp.concatenate(ys, axis=1) + xs_c * drep_ref[...]
        ug = y_c * _silu(z[rows])
        parts = []
        for g in range(SSM_GROUPS):
            blk = ug[:, g * GROUP_W:(g + 1) * GROUP_W]
            ms_g = jnp.mean(blk * blk, axis=-1, keepdims=True)
            parts.append(blk * lax.rsqrt(ms_g + NORM_EPS))
        yn = jnp.concatenate(parts, axis=1) * snw_ref[...]
        out_ssm.append(out_proj(yn, D_ATTN + D_CONV))

    out = out + jnp.concatenate(out_ssm, axis=0)
    if final_norm:
        out = _rms(out, fnw_ref[...])
    hout_ref[0] = out

    if emit_state:
        kout_ref[...] = kb
        for hd in range(ATTN_HEADS):
            vTout_ref[hd] = vTb[hd]
        tsc_out_ref[...] = ext_sc[0:TAIL, :]
        tx_out_ref[...] = ext_x[0:TAIL, :]
        s_out_ref[...] = s_scr[...]


def _const_spec(shape):
    nd = len(shape)
    return pl.BlockSpec(shape, lambda b, i, _nd=nd: (0,) * _nd, pipeline_mode=pl.Buffered(1))


def _layer_spec(shape, layer):
    block = (None, shape[1], shape[2] // 128 * 128)
    return pl.BlockSpec(block, lambda b, i: (layer, 0, 0), pipeline_mode=pl.Buffered(1))


def _layer_call(h, layer, stacks, params, state, *, T, n_pad, has_prefix, emit_state, final_norm, lam_init):
    bsz, seq, _ = h.shape
    nt = seq // T
    kern = functools.partial(_layer_kernel, T=T, n_pad=n_pad, has_prefix=has_prefix,
                             emit_state=emit_state, final_norm=final_norm, lam_init=lam_init)
    ins = [h] + list(params) + list(state)
    in_specs = [pl.BlockSpec((1, T, D_MODEL), lambda b, i: (b, i, 0))]
    in_specs += [_layer_spec(a.shape, layer) if any(a is s for s in stacks) else _const_spec(a.shape)
                 for a in params]
    in_specs += [_const_spec(a.shape) for a in state]
    out_shape = [jax.ShapeDtypeStruct((bsz, seq, D_MODEL), F32)]
    out_specs = [pl.BlockSpec((1, T, D_MODEL), lambda b, i: (b, i, 0))]
    if emit_state:
        st_shapes = [((T, D_ATTN), BF16), ((ATTN_HEADS, V_ROWS, T), BF16), ((TAIL, D_CONV), F32),
                     ((TAIL, SSM_CONV_DIM), F32), ((SSM_GROUPS, SSM_STATE, GROUP_W), F32)]
        for shp, dt_ in st_shapes:
            out_shape.append(jax.ShapeDtypeStruct(shp, dt_))
            out_specs.append(pl.BlockSpec(shp, lambda b, i, _nd=len(shp): (0,) * _nd))
    scratch = [
        pltpu.VMEM((seq, D_ATTN), BF16),
        pltpu.VMEM((nt, ATTN_HEADS, V_ROWS, T), BF16),
        pltpu.VMEM((ATTN_HEADS, 2 * T, ATTN_V_DIM), BF16),
        pltpu.VMEM((ATTN_HEADS, V_ROWS, 2 * T), F32),
        pltpu.VMEM((ATTN_HEADS, T, 2 * T), F32),
        pltpu.VMEM((TAIL + T, D_CONV), F32),
        pltpu.VMEM((TAIL + T, SSM_CONV_DIM), F32),
        pltpu.VMEM((SSM_GROUPS, SSM_STATE, GROUP_W), F32),
    ]
    return pl.pallas_call(
        kern,
        grid=(bsz, nt),
        in_specs=in_specs,
        out_specs=out_specs,
        out_shape=out_shape,
        scratch_shapes=scratch,
        compiler_params=pltpu.CompilerParams(
            dimension_semantics=("arbitrary", "arbitrary"),
            vmem_limit_bytes=VMEM_LIMIT_BYTES),
        name="layer_meta" if emit_state else "layer_main",
    )(*ins)


def kernel(x, meta_tokens, norm_w, w_in, w_out, attn_lambda, attn_norm_w, sconv_w, ssm_conv_w,
           ssm_conv_b, ssm_a_log, ssm_dt_bias, ssm_d, ssm_norm_w, final_norm_w):
    pad_heads = CHUNK - SSM_HEADS
    h_meta = jnp.concatenate([jnp.zeros((N_PAD, D_MODEL), F32), meta_tokens.astype(F32)], axis=0)[None]
    h_main = x.astype(F32)
    zero_state = (
        jnp.zeros((CHUNK, D_ATTN), BF16), jnp.zeros((ATTN_HEADS, V_ROWS, CHUNK), BF16),
        jnp.zeros((TAIL, D_CONV), F32), jnp.zeros((TAIL, SSM_CONV_DIM), F32),
        jnp.zeros((SSM_GROUPS, SSM_STATE, GROUP_W), F32),
    )
    fnw = final_norm_w.astype(F32)[None, :]
    w_in_b = w_in.astype(BF16)
    w_out_b = w_out.astype(BF16)
    stacks = (w_in_b, w_out_b)
    for layer in range(DEPTH):
        lam_init = 0.8 - 0.6 * math.exp(-0.3 * layer)
        w_dt = w_in_b[layer, :, OFF_DT:OFF_DT + SSM_HEADS]
        params = (
            norm_w[layer].astype(F32)[None, :],
            w_in_b,
            jnp.pad(w_dt, ((0, 0), (0, pad_heads))),
            w_dt.T,
            w_out_b,
            attn_lambda[layer].astype(F32),
            attn_norm_w[layer].astype(F32)[:, None],
            sconv_w[layer].astype(F32),
            ssm_conv_w[layer].astype(F32),
            ssm_conv_b[layer].astype(F32)[None, :],
            jnp.pad(ssm_a_log[layer].astype(F32), (0, pad_heads), constant_values=-jnp.inf)[None, :],
            ssm_a_log[layer].astype(F32)[:, None],
            jnp.pad(ssm_dt_bias[layer].astype(F32), (0, pad_heads))[None, :],
            ssm_dt_bias[layer].astype(F32)[:, None],
            jnp.repeat(ssm_d[layer].astype(F32), SSM_HEAD_DIM)[None, :],
            ssm_norm_w[layer].astype(F32)[None, :],
            fnw,
        )
        meta_out = _layer_call(h_meta, layer, stacks, params, zero_state, T=CHUNK, n_pad=N_PAD,
                               has_prefix=False, emit_state=True, final_norm=False, lam_init=lam_init)
        h_meta, meta_state = meta_out[0], tuple(meta_out[1:])
        (h_main,) = _layer_call(h_main, layer, stacks, params, meta_state, T=MAIN_TILE, n_pad=0,
                                has_prefix=True, emit_state=False, final_norm=(layer == DEPTH - 1),
                                lam_init=lam_init)
    return h_main.astype(x.dtype)
```

```python
import functools
import math

import jax
import jax.numpy as jnp
from jax import lax
from jax.experimental import pallas as pl
from jax.experimental.pallas import tpu as pltpu

D_MODEL = 1024
DEPTH = 4
N_META = 16
D_ATTN = 512
D_CONV = 512
D_SSM = 1024
ATTN_HEADS = 4
ATTN_QK_DIM = 64
ATTN_V_DIM = 128
SCONV_WIDTH = 3
SSM_HEAD_DIM = 64
SSM_HEADS = 16
SSM_GROUPS = 4
SSM_STATE = 128
SSM_CONV_WIDTH = 4
SSM_CONV_DIM = 2048
CHUNK = 128
N_PAD = CHUNK - N_META
NORM_EPS = 1e-5
MASK_VALUE = -1e30
LOG2_E = 1.4426950408889634
V_ROWS = ATTN_V_DIM + 16
GROUP_W = D_SSM // SSM_GROUPS
HEADS_PER_GROUP = SSM_HEADS // SSM_GROUPS

OFF_QKV = 0
OFF_GATE_A = 3 * D_ATTN
OFF_SC = 4 * D_ATTN
OFF_Z = OFF_SC + 4 * D_CONV
OFF_XBC = OFF_Z + D_SSM
OFF_DT = OFF_XBC + SSM_CONV_DIM

TAIL = 8
MAIN_TILE = 256
VMEM_LIMIT_BYTES = 56 * 1024 * 1024

F32 = jnp.float32
BF16 = jnp.bfloat16
NT_DIMS = (((1,), (1,)), ((), ()))


def _silu(x):
    half = 0.5 * x
    return half * jnp.tanh(half) + half


def _softplus(x):
    return jnp.maximum(x, 0.0) + jnp.log1p(jnp.exp(-jnp.abs(x)))


def _rms(x, w):
    ms = jnp.mean(x * x, axis=-1, keepdims=True)
    return x * lax.rsqrt(ms + NORM_EPS) * w


def _attn_scores(q_scr, k_blk_fn):
    return [lax.dot_general(k_blk_fn(hd), q_scr[hd], NT_DIMS, preferred_element_type=F32)
            for hd in range(ATTN_HEADS)]


def _attn_softmax(s2s, maxes, mask, first, T):
    new, ps, alphas = [], [], []
    for hd in range(ATTN_HEADS):
        for mp in range(2):
            s = s2s[hd][:, mp * T:(mp + 1) * T]
            if mask is not None:
                s = jnp.where(mask, s, MASK_VALUE)
            m_new = jnp.max(s, axis=0, keepdims=True)
            if not first:
                m_old = maxes[2 * hd + mp]
                m_new = jnp.maximum(m_old, m_new)
                alphas.append(jnp.exp2(m_old - m_new))
            ps.append(jnp.exp2(s - m_new).astype(BF16))
            new.append(m_new)
    return tuple(new), ps, alphas


def _attn_accumulate(acc_scr, ps, alphas, vT_blk_fn, first):
    for hd in range(ATTN_HEADS):
        pv = jnp.dot(vT_blk_fn(hd), jnp.concatenate(ps[2 * hd:2 * hd + 2], axis=1),
                     preferred_element_type=F32)
        if first:
            acc_scr[hd] = pv
        else:
            acc_scr[hd] = jnp.concatenate(alphas[2 * hd:2 * hd + 2], axis=1) * acc_scr[hd] + pv


def _layer_kernel(h_ref, nw_ref, win_ref, wdt_ref, wdtT_ref, wout_ref, lam_ref, anw_ref,
                  scw_ref, xw_ref, xb_ref, alog_row_ref, alog_col_ref, dtb_row_ref, dtb_col_ref,
                  drep_ref, snw_ref, fnw_ref, kpre_ref, vpreT_ref, tsc_ref, tx_ref, s0_ref,
                  *rest, T, n_pad, has_prefix, emit_state, final_norm, lam_init):
    if emit_state:
        (hout_ref, kout_ref, vTout_ref, tsc_out_ref, tx_out_ref, s_out_ref,
         k_scr, vT_scr, q_scr, acc_scr, s2_scr, ext_sc, ext_x, s_scr) = rest
    else:
        hout_ref, k_scr, vT_scr, q_scr, acc_scr, s2_scr, ext_sc, ext_x, s_scr = rest

    i = pl.program_id(1)
    Q = CHUNK

    @pl.when(i == 0)
    def _init():
        ext_sc[0:TAIL, :] = tsc_ref[...]
        ext_x[0:TAIL, :] = tx_ref[...]
        s_scr[...] = s0_ref[...]

    if n_pad:
        row_ok = lax.broadcasted_iota(jnp.int32, (T, 1), 0) >= n_pad
        col_ok = lax.broadcasted_iota(jnp.int32, (1, T), 1) >= n_pad

    h = h_ref[0]
    u = _rms(h, nw_ref[...])
    if n_pad:
        u = jnp.where(row_ok, u, 0.0)
    ub = u.astype(BF16)

    def proj(off, width):
        return jnp.dot(ub, win_ref[:, off:off + width], preferred_element_type=F32)

    def out_proj(y, off):
        return jnp.dot(y.astype(BF16), wout_ref[off:off + y.shape[1], :], preferred_element_type=F32)

    def causal_conv(raw, ext_ref, cols, w_ref, width, bias):
        ext_ref[TAIL:TAIL + T, cols] = raw
        acc = raw * w_ref[width - 1:width, cols]
        if bias is not None:
            acc = acc + bias
        for j in range(width - 1):
            shift = width - 1 - j
            acc = acc + ext_ref[TAIL - shift:TAIL - shift + T, cols] * w_ref[j:j + 1, cols]
        ext_ref[0:TAIL, cols] = ext_ref[T:T + TAIL, cols]
        return acc

    def head_cols(hd):
        return slice(hd * ATTN_V_DIM, (hd + 1) * ATTN_V_DIM)

    qkv = proj(OFF_QKV, 3 * D_ATTN)
    sc = proj(OFF_SC, 4 * D_CONV)
    kb = qkv[:, D_ATTN:2 * D_ATTN].astype(BF16)
    vT = qkv[:, 2 * D_ATTN:3 * D_ATTN].T
    ones_rows = (lax.broadcasted_iota(jnp.int32, (V_ROWS - ATTN_V_DIM, T), 0) == 0).astype(BF16)
    vTb = [jnp.concatenate([vT[hd * ATTN_V_DIM:(hd + 1) * ATTN_V_DIM].astype(BF16), ones_rows], axis=0)
           for hd in range(ATTN_HEADS)]
    k_scr[pl.ds(pl.multiple_of(i * T, CHUNK), T), :] = kb
    for hd in range(ATTN_HEADS):
        vT_scr[i, hd] = vTb[hd]

    q = qkv[:, 0:D_ATTN] * (ATTN_QK_DIM ** -0.5 * LOG2_E)
    lane = lax.broadcasted_iota(jnp.int32, (T, ATTN_V_DIM), 1)
    for hd in range(ATTN_HEADS):
        qh = q[:, head_cols(hd)]
        q_scr[hd, 0:T, :] = jnp.where(lane < ATTN_QK_DIM, qh, 0.0).astype(BF16)
        q_scr[hd, T:2 * T, :] = jnp.where(lane >= ATTN_QK_DIM, qh, 0.0).astype(BF16)

    cch = sc[:, D_CONV:2 * D_CONV] * sc[:, 2 * D_CONV:3 * D_CONV]
    conv = causal_conv(cch, ext_sc, slice(0, D_CONV), scw_ref, SCONV_WIDTH, None)
    y_conv = sc[:, 0:D_CONV] * conv * _silu(sc[:, 3 * D_CONV:4 * D_CONV])

    def tile_keys(j):
        off = pl.multiple_of(j * T, CHUNK)
        return lambda hd: k_scr[pl.ds(off, T), head_cols(hd)]

    def stash_scores(s2s):
        for hd in range(ATTN_HEADS):
            s2_scr[hd] = s2s[hd]

    maxes = None
    if has_prefix:
        prefix_mask = lax.broadcasted_iota(jnp.int32, (CHUNK, T), 0) >= N_PAD
        s2p = _attn_scores(q_scr, lambda hd: kpre_ref[:, head_cols(hd)])
        maxes, ps, alphas = _attn_softmax(s2p, None, prefix_mask, True, T)
        s2n = _attn_scores(q_scr, tile_keys(0))
        _attn_accumulate(acc_scr, ps, alphas, lambda hd: vpreT_ref[hd], True)
        stash_scores(s2n)

        def body(j, mx):
            new, ps, alphas = _attn_softmax([s2_scr[hd] for hd in range(ATTN_HEADS)], mx, None, False, T)
            s2n = _attn_scores(q_scr, tile_keys(j + 1))
            _attn_accumulate(acc_scr, ps, alphas, lambda hd: vT_scr[j, hd], False)
            stash_scores(s2n)
            return new

        maxes = lax.fori_loop(0, i, body, maxes)
        s2d = [s2_scr[hd] for hd in range(ATTN_HEADS)]
    else:
        s2d = _attn_scores(q_scr, lambda hd: kb[:, head_cols(hd)])

    key_i = lax.broadcasted_iota(jnp.int32, (T, T), 0)
    qry_i = lax.broadcasted_iota(jnp.int32, (T, T), 1)
    diag_mask = key_i <= qry_i
    if n_pad:
        diag_mask = jnp.logical_and(diag_mask, key_i >= n_pad)
    _, ps, alphas = _attn_softmax(s2d, maxes, diag_mask, not has_prefix, T)
    xs_raw = proj(OFF_XBC, D_SSM)
    _attn_accumulate(acc_scr, ps, alphas, lambda hd: vTb[hd], not has_prefix)
    bc_raw = proj(OFF_XBC + D_SSM, SSM_CONV_DIM - D_SSM)
    xs = _silu(causal_conv(xs_raw, ext_x, slice(0, D_SSM), xw_ref, SSM_CONV_WIDTH, xb_ref[:, 0:D_SSM]))
    gate_a = proj(OFF_GATE_A, D_ATTN)
    z = proj(OFF_Z, D_SSM)
    bc = _silu(causal_conv(bc_raw, ext_x, slice(D_SSM, SSM_CONV_DIM), xw_ref, SSM_CONV_WIDTH,
                           xb_ref[:, D_SSM:SSM_CONV_DIM]))
    if n_pad:
        xs = jnp.where(row_ok, xs, 0.0)
        bc = jnp.where(row_ok, bc, 0.0)

    lp = lam_ref[...]
    lam = (jnp.exp(jnp.sum(lp[0:1] * lp[1:2], axis=-1, keepdims=True))
           - jnp.exp(jnp.sum(lp[2:3] * lp[3:4], axis=-1, keepdims=True)) + lam_init)
    y_attn = []
    for hd in range(ATTN_HEADS):
        dv = ATTN_V_DIM
        inv_l0 = 1.0 / acc_scr[hd, dv:dv + 1, 0:T]
        inv_l1 = 1.0 / acc_scr[hd, dv:dv + 1, T:2 * T]
        oT = acc_scr[hd, 0:dv, 0:T] * inv_l0 - lam * (acc_scr[hd, 0:dv, T:2 * T] * inv_l1)
        ms = jnp.mean(oT * oT, axis=0, keepdims=True)
        oT = oT * lax.rsqrt(ms + NORM_EPS) * anw_ref[...] * (1.0 - lam_init)
        y_attn.append(oT.T * _silu(gate_a[:, head_cols(hd)]))
    out = h + out_proj(y_conv, D_ATTN)
    out = out + out_proj(jnp.concatenate(y_attn, axis=1), 0)

    dt = _softplus(jnp.dot(ub, wdt_ref[...], preferred_element_type=F32) + dtb_row_ref[...])
    dtT = _softplus(lax.dot_general(wdtT_ref[...], ub, NT_DIMS, preferred_element_type=F32)
                    + dtb_col_ref[...])
    if n_pad:
        dt = jnp.where(row_ok, dt, 0.0)
        dtT = jnp.where(col_ok, dtT, 0.0)
    adt = dt * (-jnp.exp(alog_row_ref[...]))
    adtT = dtT * (-jnp.exp(alog_col_ref[...]))

    qr = lax.broadcasted_iota(jnp.int32, (Q, Q), 0)
    qc = lax.broadcasted_iota(jnp.int32, (Q, Q), 1)
    tril = qr >= qc
    lower_ones = tril.astype(F32)
    upper_ones = (qr <= qc).astype(F32)
    expand = (lax.broadcasted_iota(jnp.int32, (CHUNK, D_SSM), 0)
              == (lax.broadcasted_iota(jnp.int32, (CHUNK, D_SSM), 1) // SSM_HEAD_DIM)).astype(F32)
    colblk = lax.broadcasted_iota(jnp.int32, (Q, GROUP_W), 1) // SSM_HEAD_DIM
    head_mask = [jnp.where(colblk == hh, 1.0, 0.0).astype(BF16) for hh in range(HEADS_PER_GROUP)]
    low_half = lax.broadcasted_iota(jnp.int32, (Q, CHUNK), 1) < SSM_HEAD_DIM

    out_ssm = []
    for c in range(T // Q):
        rows = slice(c * Q, (c + 1) * Q)
        acum = jnp.dot(lower_ones, adt[rows], preferred_element_type=F32,
                       precision=lax.Precision.HIGHEST)
        acumT = jnp.dot(adtT[:, rows], upper_ones, preferred_element_type=F32,
                        precision=lax.Precision.HIGHEST)
        dtT_c = dtT[:, rows]
        wT = jnp.exp(acumT[:, Q - 1:Q] - acumT) * dtT_c
        total8 = jnp.broadcast_to(acum[Q - 1:Q, :], (8, CHUNK))
        chunk_decay = jnp.exp(jnp.dot(total8, expand, preferred_element_type=F32,
                                      precision=lax.Precision.HIGHEST))[0:1, :]
        xs_c = xs[rows]
        b_gs = [bc[rows, g * SSM_STATE:(g + 1) * SSM_STATE] for g in range(SSM_GROUPS)]
        c_gs = [bc[rows, (SSM_GROUPS + g) * SSM_STATE:(SSM_GROUPS + g + 1) * SSM_STATE].astype(BF16)
                for g in range(SSM_GROUPS)]
        s_prevs = [s_scr[g] for g in range(SSM_GROUPS)]
        cbs = [lax.dot_general(c_gs[g], b_gs[g].astype(BF16), NT_DIMS, preferred_element_type=F32)
               for g in range(SSM_GROUPS)]
        y_offs = [jnp.dot(c_gs[g], s_prevs[g].astype(BF16), preferred_element_type=F32)
                  for g in range(SSM_GROUPS)]
        lhs_diag, lhs_state, x_bds, e_gs = [], [], [], []
        for g in range(SSM_GROUPS):
            gcols = slice(g * GROUP_W, (g + 1) * GROUP_W)
            b_gT = b_gs[g].T
            xgb = xs_c[:, gcols].astype(BF16)
            x_bds.append(jnp.concatenate([xgb * head_mask[hh] for hh in range(HEADS_PER_GROUP)], axis=0))
            ms, bws, eacs = [], [], []
            for hh in range(HEADS_PER_GROUP):
                hidx = g * HEADS_PER_GROUP + hh
                acol = jnp.broadcast_to(acum[:, hidx:hidx + 1], (Q, Q))
                arow = acumT[hidx:hidx + 1, :]
                decay = jnp.exp(jnp.where(tril, acol - arow, MASK_VALUE))
                ms.append((cbs[g] * decay * dtT_c[hidx:hidx + 1, :]).astype(BF16))
                bws.append((b_gT * wT[hidx:hidx + 1, :]).astype(BF16))
                eacs.append(jnp.exp(acol))
            e_gs.append(jnp.concatenate([jnp.where(low_half, eacs[0], eacs[1]),
                                         jnp.where(low_half, eacs[2], eacs[3])], axis=1))
            lhs_diag.append(jnp.concatenate(ms, axis=1))
            lhs_state.append(jnp.concatenate(bws, axis=1))
        y_diags = [jnp.dot(lhs_diag[g], x_bds[g], preferred_element_type=F32) for g in range(SSM_GROUPS)]
        s_incs = [jnp.dot(lhs_state[g], x_bds[g], preferred_element_type=F32) for g in range(SSM_GROUPS)]
        ys = []
        for g in range(SSM_GROUPS):
            gcols = slice(g * GROUP_W, (g + 1) * GROUP_W)
            ys.append(y_diags[g] + y_offs[g] * e_gs[g])
            s_scr[g] = chunk_decay[:, gcols] * s_prevs[g] + s_incs[g]
        y_c = jnp.concatenate(ys, axis=1) + xs_c * drep_ref[...]
        ug = y_c * _silu(z[rows])
        parts = []
        for g in range(SSM_GROUPS):
            blk = ug[:, g * GROUP_W:(g + 1) * GROUP_W]
            ms_g = jnp.mean(blk * blk, axis=-1, keepdims=True)
            parts.append(blk * lax.rsqrt(ms_g + NORM_EPS))
        yn = jnp.concatenate(parts, axis=1) * snw_ref[...]
        out_ssm.append(out_proj(yn, D_ATTN + D_CONV))

    out = out + jnp.concatenate(out_ssm, axis=0)
    if final_norm:
        out = _rms(out, fnw_ref[...])
    hout_ref[0] = out

    if emit_state:
        kout_ref[...] = kb
        for hd in range(ATTN_HEADS):
            vTout_ref[hd] = vTb[hd]
        tsc_out_ref[...] = ext_sc[0:TAIL, :]
        tx_out_ref[...] = ext_x[0:TAIL, :]
        s_out_ref[...] = s_scr[...]


def _const_spec(shape):
    nd = len(shape)
    return pl.BlockSpec(shape, lambda b, i, _nd=nd: (0,) * _nd, pipeline_mode=pl.Buffered(1))


def _stack_spec(shape, layer):
    return pl.BlockSpec((None,) + tuple(shape[1:]), lambda b, i: (layer, 0, 0), pipeline_mode=pl.Buffered(1))


def _layer_call(h, layer, params, state, *, T, n_pad, has_prefix, emit_state, final_norm, lam_init):
    bsz, seq, _ = h.shape
    nt = seq // T
    kern = functools.partial(_layer_kernel, T=T, n_pad=n_pad, has_prefix=has_prefix,
                             emit_state=emit_state, final_norm=final_norm, lam_init=lam_init)
    ins = [h] + list(params) + list(state)
    in_specs = [pl.BlockSpec((1, T, D_MODEL), lambda b, i: (b, i, 0))]
    in_specs += [_stack_spec(a.shape, layer) if a.ndim == 3 else _const_spec(a.shape) for a in params]
    in_specs += [_const_spec(a.shape) for a in state]
    out_shape = [jax.ShapeDtypeStruct((bsz, seq, D_MODEL), F32)]
    out_specs = [pl.BlockSpec((1, T, D_MODEL), lambda b, i: (b, i, 0))]
    if emit_state:
        st_shapes = [((T, D_ATTN), BF16), ((ATTN_HEADS, V_ROWS, T), BF16), ((TAIL, D_CONV), F32),
                     ((TAIL, SSM_CONV_DIM), F32), ((SSM_GROUPS, SSM_STATE, GROUP_W), F32)]
        for shp, dt_ in st_shapes:
            out_shape.append(jax.ShapeDtypeStruct(shp, dt_))
            out_specs.append(pl.BlockSpec(shp, lambda b, i, _nd=len(shp): (0,) * _nd))
    scratch = [
        pltpu.VMEM((seq, D_ATTN), BF16),
        pltpu.VMEM((nt, ATTN_HEADS, V_ROWS, T), BF16),
        pltpu.VMEM((ATTN_HEADS, 2 * T, ATTN_V_DIM), BF16),
        pltpu.VMEM((ATTN_HEADS, V_ROWS, 2 * T), F32),
        pltpu.VMEM((ATTN_HEADS, T, 2 * T), F32),
        pltpu.VMEM((TAIL + T, D_CONV), F32),
        pltpu.VMEM((TAIL + T, SSM_CONV_DIM), F32),
        pltpu.VMEM((SSM_GROUPS, SSM_STATE, GROUP_W), F32),
    ]
    return pl.pallas_call(
        kern,
        grid=(bsz, nt),
        in_specs=in_specs,
        out_specs=out_specs,
        out_shape=out_shape,
        scratch_shapes=scratch,
        compiler_params=pltpu.CompilerParams(
            dimension_semantics=("arbitrary", "arbitrary"),
            vmem_limit_bytes=VMEM_LIMIT_BYTES),
        name="layer_meta" if emit_state else "layer_main",
    )(*ins)


def kernel(x, meta_tokens, norm_w, w_in, w_out, attn_lambda, attn_norm_w, sconv_w, ssm_conv_w,
           ssm_conv_b, ssm_a_log, ssm_dt_bias, ssm_d, ssm_norm_w, final_norm_w):
    pad_heads = CHUNK - SSM_HEADS
    h_meta = jnp.concatenate([jnp.zeros((N_PAD, D_MODEL), F32), meta_tokens.astype(F32)], axis=0)[None]
    h_main = x.astype(F32)
    zero_state = (
        jnp.zeros((CHUNK, D_ATTN), BF16), jnp.zeros((ATTN_HEADS, V_ROWS, CHUNK), BF16),
        jnp.zeros((TAIL, D_CONV), F32), jnp.zeros((TAIL, SSM_CONV_DIM), F32),
        jnp.zeros((SSM_GROUPS, SSM_STATE, GROUP_W), F32),
    )
    fnw = final_norm_w.astype(F32)[None, :]
    w_in_b = w_in[:, :, :OFF_DT].astype(BF16)
    w_out_b = w_out.astype(BF16)
    for layer in range(DEPTH):
        lam_init = 0.8 - 0.6 * math.exp(-0.3 * layer)
        w_dt = w_in[layer, :, OFF_DT:OFF_DT + SSM_HEADS]
        params = (
            norm_w[layer].astype(F32)[None, :],
            w_in_b,
            jnp.pad(w_dt, ((0, 0), (0, pad_heads))).astype(BF16),
            w_dt.T.astype(BF16),
            w_out_b,
            attn_lambda[layer].astype(F32),
            attn_norm_w[layer].astype(F32)[:, None],
            sconv_w[layer].astype(F32),
            ssm_conv_w[layer].astype(F32),
            ssm_conv_b[layer].astype(F32)[None, :],
            jnp.pad(ssm_a_log[layer].astype(F32), (0, pad_heads), constant_values=-jnp.inf)[None, :],
            ssm_a_log[layer].astype(F32)[:, None],
            jnp.pad(ssm_dt_bias[layer].astype(F32), (0, pad_heads))[None, :],
            ssm_dt_bias[layer].astype(F32)[:, None],
            jnp.repeat(ssm_d[layer].astype(F32), SSM_HEAD_DIM)[None, :],
            ssm_norm_w[layer].astype(F32)[None, :],
            fnw,
        )
        meta_out = _layer_call(h_meta, layer, params, zero_state, T=CHUNK, n_pad=N_PAD, has_prefix=False,
                               emit_state=True, final_norm=False, lam_init=lam_init)
        h_meta, meta_state = meta_out[0], tuple(meta_out[1:])
        (h_main,) = _layer_call(h_main, layer, params, meta_state, T=MAIN_TILE, n_pad=0, has_prefix=True,
                                emit_state=False, final_norm=(layer == DEPTH - 1), lam_init=lam_init)
    return h_main.astype(x.dtype)
```

```python
import functools
import math

import jax
import jax.numpy as jnp
from jax import lax
from jax.experimental import pallas as pl
from jax.experimental.pallas import tpu as pltpu

D_MODEL = 1024
DEPTH = 4
N_META = 16
D_ATTN = 512
D_CONV = 512
D_SSM = 1024
ATTN_HEADS = 4
ATTN_QK_DIM = 64
ATTN_V_DIM = 128
SCONV_WIDTH = 3
SSM_HEAD_DIM = 64
SSM_HEADS = 16
SSM_GROUPS = 4
SSM_STATE = 128
SSM_CONV_WIDTH = 4
SSM_CONV_DIM = 2048
CHUNK = 128
N_PAD = CHUNK - N_META
NORM_EPS = 1e-5
MASK_VALUE = -1e30
LOG2_E = 1.4426950408889634
V_ROWS = ATTN_V_DIM + 16
GROUP_W = D_SSM // SSM_GROUPS
HEADS_PER_GROUP = SSM_HEADS // SSM_GROUPS

OFF_QKV = 0
OFF_GATE_A = 3 * D_ATTN
OFF_SC = 4 * D_ATTN
OFF_Z = OFF_SC + 4 * D_CONV
OFF_XBC = OFF_Z + D_SSM
OFF_DT = OFF_XBC + SSM_CONV_DIM

TAIL = 8
MAIN_TILE = 256
VMEM_LIMIT_BYTES = 56 * 1024 * 1024

F32 = jnp.float32
BF16 = jnp.bfloat16
NT_DIMS = (((1,), (1,)), ((), ()))


def _silu(x):
    half = 0.5 * x
    return half * jnp.tanh(half) + half


def _softplus(x):
    return jnp.maximum(x, 0.0) + jnp.log1p(jnp.exp(-jnp.abs(x)))


def _rms(x, w):
    ms = jnp.mean(x * x, axis=-1, keepdims=True)
    return x * lax.rsqrt(ms + NORM_EPS) * w


def _attn_scores(q_scr, k_blk_fn):
    return [lax.dot_general(k_blk_fn(hd), q_scr[hd], NT_DIMS, preferred_element_type=F32)
            for hd in range(ATTN_HEADS)]


def _attn_softmax(s2s, maxes, mask, first, T):
    new, ps, alphas = [], [], []
    for hd in range(ATTN_HEADS):
        for mp in range(2):
            s = s2s[hd][:, mp * T:(mp + 1) * T]
            if mask is not None:
                s = jnp.where(mask, s, MASK_VALUE)
            m_new = jnp.max(s, axis=0, keepdims=True)
            if not first:
                m_old = maxes[2 * hd + mp]
                m_new = jnp.maximum(m_old, m_new)
                alphas.append(jnp.exp2(m_old - m_new))
            ps.append(jnp.exp2(s - m_new).astype(BF16))
            new.append(m_new)
    return tuple(new), ps, alphas


def _attn_accumulate(acc_scr, ps, alphas, vT_blk_fn, first):
    for hd in range(ATTN_HEADS):
        pv = jnp.dot(vT_blk_fn(hd), jnp.concatenate(ps[2 * hd:2 * hd + 2], axis=1),
                     preferred_element_type=F32)
        if first:
            acc_scr[hd] = pv
        else:
            acc_scr[hd] = jnp.concatenate(alphas[2 * hd:2 * hd + 2], axis=1) * acc_scr[hd] + pv


def _layer_kernel(h_ref, nw_ref, win_ref, wdt_ref, wdtT_ref, wout_ref, lam_ref, anw_ref,
                  scw_ref, xw_ref, xb_ref, alog_row_ref, alog_col_ref, dtb_row_ref, dtb_col_ref,
                  drep_ref, snw_ref, fnw_ref, kpre_ref, vpreT_ref, tsc_ref, tx_ref, s0_ref,
                  *rest, T, n_pad, has_prefix, emit_state, final_norm, lam_init):
    if emit_state:
        (hout_ref, kout_ref, vTout_ref, tsc_out_ref, tx_out_ref, s_out_ref,
         k_scr, vT_scr, q_scr, acc_scr, s2_scr, ext_sc, ext_x, s_scr) = rest
    else:
        hout_ref, k_scr, vT_scr, q_scr, acc_scr, s2_scr, ext_sc, ext_x, s_scr = rest

    i = pl.program_id(1)
    Q = CHUNK

    @pl.when(i == 0)
    def _init():
        ext_sc[0:TAIL, :] = tsc_ref[...]
        ext_x[0:TAIL, :] = tx_ref[...]
        s_scr[...] = s0_ref[...]

    if n_pad:
        row_ok = lax.broadcasted_iota(jnp.int32, (T, 1), 0) >= n_pad
        col_ok = lax.broadcasted_iota(jnp.int32, (1, T), 1) >= n_pad

    h = h_ref[0]
    u = _rms(h, nw_ref[...])
    if n_pad:
        u = jnp.where(row_ok, u, 0.0)
    ub = u.astype(BF16)

    def proj(off, width):
        return jnp.dot(ub, win_ref[:, off:off + width], preferred_element_type=F32)

    def out_proj(y, off):
        return jnp.dot(y.astype(BF16), wout_ref[off:off + y.shape[1], :], preferred_element_type=F32)

    def causal_conv(raw, ext_ref, cols, w_ref, width, bias):
        ext_ref[TAIL:TAIL + T, cols] = raw
        acc = raw * w_ref[width - 1:width, cols]
        if bias is not None:
            acc = acc + bias
        for j in range(width - 1):
            shift = width - 1 - j
            acc = acc + ext_ref[TAIL - shift:TAIL - shift + T, cols] * w_ref[j:j + 1, cols]
        ext_ref[0:TAIL, cols] = ext_ref[T:T + TAIL, cols]
        return acc

    def head_cols(hd):
        return slice(hd * ATTN_V_DIM, (hd + 1) * ATTN_V_DIM)

    qkv = proj(OFF_QKV, 3 * D_ATTN)
    sc = proj(OFF_SC, 4 * D_CONV)
    kb = qkv[:, D_ATTN:2 * D_ATTN].astype(BF16)
    vT = qkv[:, 2 * D_ATTN:3 * D_ATTN].T
    ones_rows = (lax.broadcasted_iota(jnp.int32, (V_ROWS - ATTN_V_DIM, T), 0) == 0).astype(BF16)
    vTb = [jnp.concatenate([vT[hd * ATTN_V_DIM:(hd + 1) * ATTN_V_DIM].astype(BF16), ones_rows], axis=0)
           for hd in range(ATTN_HEADS)]
    k_scr[pl.ds(pl.multiple_of(i * T, CHUNK), T), :] = kb
    for hd in range(ATTN_HEADS):
        vT_scr[i, hd] = vTb[hd]

    q = qkv[:, 0:D_ATTN] * (ATTN_QK_DIM ** -0.5 * LOG2_E)
    lane = lax.broadcasted_iota(jnp.int32, (T, ATTN_V_DIM), 1)
    for hd in range(ATTN_HEADS):
        qh = q[:, head_cols(hd)]
        q_scr[hd, 0:T, :] = jnp.where(lane < ATTN_QK_DIM, qh, 0.0).astype(BF16)
        q_scr[hd, T:2 * T, :] = jnp.where(lane >= ATTN_QK_DIM, qh, 0.0).astype(BF16)

    cch = sc[:, D_CONV:2 * D_CONV] * sc[:, 2 * D_CONV:3 * D_CONV]
    conv = causal_conv(cch, ext_sc, slice(0, D_CONV), scw_ref, SCONV_WIDTH, None)
    y_conv = sc[:, 0:D_CONV] * conv * _silu(sc[:, 3 * D_CONV:4 * D_CONV])

    def tile_keys(j):
        off = pl.multiple_of(j * T, CHUNK)
        return lambda hd: k_scr[pl.ds(off, T), head_cols(hd)]

    def stash_scores(s2s):
        for hd in range(ATTN_HEADS):
            s2_scr[hd] = s2s[hd]

    maxes = None
    if has_prefix:
        prefix_mask = lax.broadcasted_iota(jnp.int32, (CHUNK, T), 0) >= N_PAD
        s2p = _attn_scores(q_scr, lambda hd: kpre_ref[:, head_cols(hd)])
        maxes, ps, alphas = _attn_softmax(s2p, None, prefix_mask, True, T)
        s2n = _attn_scores(q_scr, tile_keys(0))
        _attn_accumulate(acc_scr, ps, alphas, lambda hd: vpreT_ref[hd], True)
        stash_scores(s2n)

        def body(j, mx):
            new, ps, alphas = _attn_softmax([s2_scr[hd] for hd in range(ATTN_HEADS)], mx, None, False, T)
            s2n = _attn_scores(q_scr, tile_keys(j + 1))
            _attn_accumulate(acc_scr, ps, alphas, lambda hd: vT_scr[j, hd], False)
            stash_scores(s2n)
            return new

        maxes = lax.fori_loop(0, i, body, maxes)
        s2d = [s2_scr[hd] for hd in range(ATTN_HEADS)]
    else:
        s2d = _attn_scores(q_scr, lambda hd: kb[:, head_cols(hd)])

    key_i = lax.broadcasted_iota(jnp.int32, (T, T), 0)
    qry_i = lax.broadcasted_iota(jnp.int32, (T, T), 1)
    diag_mask = key_i <= qry_i
    if n_pad:
        diag_mask = jnp.logical_and(diag_mask, key_i >= n_pad)
    _, ps, alphas = _attn_softmax(s2d, maxes, diag_mask, not has_prefix, T)
    xs_raw = proj(OFF_XBC, D_SSM)
    _attn_accumulate(acc_scr, ps, alphas, lambda hd: vTb[hd], not has_prefix)
    bc_raw = proj(OFF_XBC + D_SSM, SSM_CONV_DIM - D_SSM)
    xs = _silu(causal_conv(xs_raw, ext_x, slice(0, D_SSM), xw_ref, SSM_CONV_WIDTH, xb_ref[:, 0:D_SSM]))
    gate_a = proj(OFF_GATE_A, D_ATTN)
    z = proj(OFF_Z, D_SSM)
    bc = _silu(causal_conv(bc_raw, ext_x, slice(D_SSM, SSM_CONV_DIM), xw_ref, SSM_CONV_WIDTH,
                           xb_ref[:, D_SSM:SSM_CONV_DIM]))
    if n_pad:
        xs = jnp.where(row_ok, xs, 0.0)
        bc = jnp.where(row_ok, bc, 0.0)

    lp = lam_ref[...]
    lam = (jnp.exp(jnp.sum(lp[0:1] * lp[1:2], axis=-1, keepdims=True))
           - jnp.exp(jnp.sum(lp[2:3] * lp[3:4], axis=-1, keepdims=True)) + lam_init)
    y_attn = []
    for hd in range(ATTN_HEADS):
        dv = ATTN_V_DIM
        inv_l0 = 1.0 / acc_scr[hd, dv:dv + 1, 0:T]
        inv_l1 = 1.0 / acc_scr[hd, dv:dv + 1, T:2 * T]
        oT = acc_scr[hd, 0:dv, 0:T] * inv_l0 - lam * (acc_scr[hd, 0:dv, T:2 * T] * inv_l1)
        ms = jnp.mean(oT * oT, axis=0, keepdims=True)
        oT = oT * lax.rsqrt(ms + NORM_EPS) * anw_ref[...] * (1.0 - lam_init)
        y_attn.append(oT.T * _silu(gate_a[:, head_cols(hd)]))
    out = h + out_proj(y_conv, D_ATTN)
    out = out + out_proj(jnp.concatenate(y_attn, axis=1), 0)

    dt = _softplus(jnp.dot(ub, wdt_ref[...], preferred_element_type=F32) + dtb_row_ref[...])
    dtT = _softplus(lax.dot_general(wdtT_ref[...], ub, NT_DIMS, preferred_element_type=F32)
                    + dtb_col_ref[...])
    if n_pad:
        dt = jnp.where(row_ok, dt, 0.0)
        dtT = jnp.where(col_ok, dtT, 0.0)
    adt = dt * (-jnp.exp(alog_row_ref[...]))
    adtT = dtT * (-jnp.exp(alog_col_ref[...]))

    qr = lax.broadcasted_iota(jnp.int32, (Q, Q), 0)
    qc = lax.broadcasted_iota(jnp.int32, (Q, Q), 1)
    tril = qr >= qc
    lower_ones = tril.astype(F32)
    upper_ones = (qr <= qc).astype(F32)
    expand = (lax.broadcasted_iota(jnp.int32, (CHUNK, D_SSM), 0)
              == (lax.broadcasted_iota(jnp.int32, (CHUNK, D_SSM), 1) // SSM_HEAD_DIM)).astype(F32)
    colblk = lax.broadcasted_iota(jnp.int32, (Q, GROUP_W), 1) // SSM_HEAD_DIM
    head_mask = [jnp.where(colblk == hh, 1.0, 0.0).astype(BF16) for hh in range(HEADS_PER_GROUP)]
    low_half = lax.broadcasted_iota(jnp.int32, (Q, CHUNK), 1) < SSM_HEAD_DIM

    out_ssm = []
    for c in range(T // Q):
        rows = slice(c * Q, (c + 1) * Q)
        acum = jnp.dot(lower_ones, adt[rows], preferred_element_type=F32,
                       precision=lax.Precision.HIGHEST)
        acumT = jnp.dot(adtT[:, rows], upper_ones, preferred_element_type=F32,
                        precision=lax.Precision.HIGHEST)
        dtT_c = dtT[:, rows]
        wT = jnp.exp(acumT[:, Q - 1:Q] - acumT) * dtT_c
        total8 = jnp.broadcast_to(acum[Q - 1:Q, :], (8, CHUNK))
        chunk_decay = jnp.exp(jnp.dot(total8, expand, preferred_element_type=F32,
                                      precision=lax.Precision.HIGHEST))[0:1, :]
        xs_c = xs[rows]
        b_gs = [bc[rows, g * SSM_STATE:(g + 1) * SSM_STATE] for g in range(SSM_GROUPS)]
        c_gs = [bc[rows, (SSM_GROUPS + g) * SSM_STATE:(SSM_GROUPS + g + 1) * SSM_STATE].astype(BF16)
                for g in range(SSM_GROUPS)]
        s_prevs = [s_scr[g] for g in range(SSM_GROUPS)]
        cbs = [lax.dot_general(c_gs[g], b_gs[g].astype(BF16), NT_DIMS, preferred_element_type=F32)
               for g in range(SSM_GROUPS)]
        y_offs = [jnp.dot(c_gs[g], s_prevs[g].astype(BF16), preferred_element_type=F32)
                  for g in range(SSM_GROUPS)]
        lhs_diag, lhs_state, x_bds, e_gs = [], [], [], []
        for g in range(SSM_GROUPS):
            gcols = slice(g * GROUP_W, (g + 1) * GROUP_W)
            b_gT = b_gs[g].T
            xgb = xs_c[:, gcols].astype(BF16)
            x_bds.append(jnp.concatenate([xgb * head_mask[hh] for hh in range(HEADS_PER_GROUP)], axis=0))
            ms, bws, eacs = [], [], []
            for hh in range(HEADS_PER_GROUP):
                hidx = g * HEADS_PER_GROUP + hh
                acol = jnp.broadcast_to(acum[:, hidx:hidx + 1], (Q, Q))
                arow = acumT[hidx:hidx + 1, :]
                decay = jnp.exp(jnp.where(tril, acol - arow, MASK_VALUE))
                ms.append((cbs[g] * decay * dtT_c[hidx:hidx + 1, :]).astype(BF16))
                bws.append((b_gT * wT[hidx:hidx + 1, :]).astype(BF16))
                eacs.append(jnp.exp(acol))
            e_gs.append(jnp.concatenate([jnp.where(low_half, eacs[0], eacs[1]),
                                         jnp.where(low_half, eacs[2], eacs[3])], axis=1))
            lhs_diag.append(jnp.concatenate(ms, axis=1))
            lhs_state.append(jnp.concatenate(bws, axis=1))
        y_diags = [jnp.dot(lhs_diag[g], x_bds[g], preferred_element_type=F32) for g in range(SSM_GROUPS)]
        s_incs = [jnp.dot(lhs_state[g], x_bds[g], preferred_element_type=F32) for g in range(SSM_GROUPS)]
        ys = []
        for g in range(SSM_GROUPS):
            gcols = slice(g * GROUP_W, (g + 1) * GROUP_W)
            ys.append(y_diags[g] + y_offs[g] * e_gs[g])
            s_scr[g] = chunk_decay[:, gcols] * s_prevs[g] + s_incs[g]
        y_c = jnp.concatenate(ys, axis=1) + xs_c * drep_ref[...]
        ug = y_c * _silu(z[rows])
        parts = []
        for g in range(SSM_GROUPS):
            blk = ug[:, g * GROUP_W:(g + 1) * GROUP_W]
            ms_g = jnp.mean(blk * blk, axis=-1, keepdims=True)
            parts.append(blk * lax.rsqrt(ms_g + NORM_EPS))
        yn = jnp.concatenate(parts, axis=1) * snw_ref[...]
        out_ssm.append(out_proj(yn, D_ATTN + D_CONV))

    out = out + jnp.concatenate(out_ssm, axis=0)
    if final_norm:
        out = _rms(out, fnw_ref[...])
    hout_ref[0] = out

    if emit_state:
        kout_ref[...] = kb
        for hd in range(ATTN_HEADS):
            vTout_ref[hd] = vTb[hd]
        tsc_out_ref[...] = ext_sc[0:TAIL, :]
        tx_out_ref[...] = ext_x[0:TAIL, :]
        s_out_ref[...] = s_scr[...]


def _const_spec(shape):
    nd = len(shape)
    return pl.BlockSpec(shape, lambda b, i, _nd=nd: (0,) * _nd, pipeline_mode=pl.Buffered(1))


def _layer_call(h, params, state, *, T, n_pad, has_prefix, emit_state, final_norm, lam_init):
    bsz, seq, _ = h.shape
    nt = seq // T
    kern = functools.partial(_layer_kernel, T=T, n_pad=n_pad, has_prefix=has_prefix,
                             emit_state=emit_state, final_norm=final_norm, lam_init=lam_init)
    ins = [h] + list(params) + list(state)
    in_specs = [pl.BlockSpec((1, T, D_MODEL), lambda b, i: (b, i, 0))]
    in_specs += [_const_spec(a.shape) for a in list(params) + list(state)]
    out_shape = [jax.ShapeDtypeStruct((bsz, seq, D_MODEL), F32)]
    out_specs = [pl.BlockSpec((1, T, D_MODEL), lambda b, i: (b, i, 0))]
    if emit_state:
        st_shapes = [((T, D_ATTN), BF16), ((ATTN_HEADS, V_ROWS, T), BF16), ((TAIL, D_CONV), F32),
                     ((TAIL, SSM_CONV_DIM), F32), ((SSM_GROUPS, SSM_STATE, GROUP_W), F32)]
        for shp, dt_ in st_shapes:
            out_shape.append(jax.ShapeDtypeStruct(shp, dt_))
            out_specs.append(pl.BlockSpec(shp, lambda b, i, _nd=len(shp): (0,) * _nd))
    scratch = [
        pltpu.VMEM((seq, D_ATTN), BF16),
        pltpu.VMEM((nt, ATTN_HEADS, V_ROWS, T), BF16),
        pltpu.VMEM((ATTN_HEADS, 2 * T, ATTN_V_DIM), BF16),
        pltpu.VMEM((ATTN_HEADS, V_ROWS, 2 * T), F32),
        pltpu.VMEM((ATTN_HEADS, T, 2 * T), F32),
        pltpu.VMEM((TAIL + T, D_CONV), F32),
        pltpu.VMEM((TAIL + T, SSM_CONV_DIM), F32),
        pltpu.VMEM((SSM_GROUPS, SSM_STATE, GROUP_W), F32),
    ]
    return pl.pallas_call(
        kern,
        grid=(bsz, nt),
        in_specs=in_specs,
        out_specs=out_specs,
        out_shape=out_shape,
        scratch_shapes=scratch,
        compiler_params=pltpu.CompilerParams(
            dimension_semantics=("arbitrary", "arbitrary"),
            vmem_limit_bytes=VMEM_LIMIT_BYTES),
        name="layer_meta" if emit_state else "layer_main",
    )(*ins)


def kernel(x, meta_tokens, norm_w, w_in, w_out, attn_lambda, attn_norm_w, sconv_w, ssm_conv_w,
           ssm_conv_b, ssm_a_log, ssm_dt_bias, ssm_d, ssm_norm_w, final_norm_w):
    pad_heads = CHUNK - SSM_HEADS
    h_meta = jnp.concatenate([jnp.zeros((N_PAD, D_MODEL), F32), meta_tokens.astype(F32)], axis=0)[None]
    h_main = x.astype(F32)
    zero_state = (
        jnp.zeros((CHUNK, D_ATTN), BF16), jnp.zeros((ATTN_HEADS, V_ROWS, CHUNK), BF16),
        jnp.zeros((TAIL, D_CONV), F32), jnp.zeros((TAIL, SSM_CONV_DIM), F32),
        jnp.zeros((SSM_GROUPS, SSM_STATE, GROUP_W), F32),
    )
    fnw = final_norm_w.astype(F32)[None, :]
    for layer in range(DEPTH):
        lam_init = 0.8 - 0.6 * math.exp(-0.3 * layer)
        w_l = w_in[layer]
        w_dt = w_l[:, OFF_DT:OFF_DT + SSM_HEADS]
        params = (
            norm_w[layer].astype(F32)[None, :],
            w_l.astype(BF16),
            jnp.pad(w_dt, ((0, 0), (0, pad_heads))).astype(BF16),
            w_dt.T.astype(BF16),
            w_out[layer].astype(BF16),
            attn_lambda[layer].astype(F32),
            attn_norm_w[layer].astype(F32)[:, None],
            sconv_w[layer].astype(F32),
            ssm_conv_w[layer].astype(F32),
            ssm_conv_b[layer].astype(F32)[None, :],
            jnp.pad(ssm_a_log[layer].astype(F32), (0, pad_heads), constant_values=-jnp.inf)[None, :],
            ssm_a_log[layer].astype(F32)[:, None],
            jnp.pad(ssm_dt_bias[layer].astype(F32), (0, pad_heads))[None, :],
            ssm_dt_bias[layer].astype(F32)[:, None],
            jnp.repeat(ssm_d[layer].astype(F32), SSM_HEAD_DIM)[None, :],
            ssm_norm_w[layer].astype(F32)[None, :],
            fnw,
        )
        meta_out = _layer_call(h_meta, params, zero_state, T=CHUNK, n_pad=N_PAD, has_prefix=False,
                               emit_state=True, final_norm=False, lam_init=lam_init)
        h_meta, meta_state = meta_out[0], tuple(meta_out[1:])
        (h_main,) = _layer_call(h_main, params, meta_state, T=MAIN_TILE, n_pad=0, has_prefix=True,
                                emit_state=False, final_norm=(layer == DEPTH - 1), lam_init=lam_init)
    return h_main.astype(x.dtype)
```

```python
import functools
import math

import jax
import jax.numpy as jnp
from jax import lax
from jax.experimental import pallas as pl
from jax.experimental.pallas import tpu as pltpu

D_MODEL = 1024
DEPTH = 4
N_META = 16
D_ATTN = 512
D_CONV = 512
D_SSM = 1024
ATTN_HEADS = 4
ATTN_QK_DIM = 64
ATTN_V_DIM = 128
SCONV_WIDTH = 3
SSM_HEAD_DIM = 64
SSM_HEADS = 16
SSM_GROUPS = 4
SSM_STATE = 128
SSM_CONV_WIDTH = 4
SSM_CONV_DIM = 2048
CHUNK = 128
N_PAD = CHUNK - N_META
NORM_EPS = 1e-5
MASK_VALUE = -1e30
LOG2_E = 1.4426950408889634
V_ROWS = ATTN_V_DIM + 16
GROUP_W = D_SSM // SSM_GROUPS
HEADS_PER_GROUP = SSM_HEADS // SSM_GROUPS

OFF_QKV = 0
OFF_GATE_A = 3 * D_ATTN
OFF_SC = 4 * D_ATTN
OFF_Z = OFF_SC + 4 * D_CONV
OFF_XBC = OFF_Z + D_SSM
OFF_DT = OFF_XBC + SSM_CONV_DIM

TAIL = 8
MAIN_TILE = 256
VMEM_LIMIT_BYTES = 56 * 1024 * 1024

F32 = jnp.float32
BF16 = jnp.bfloat16
NT_DIMS = (((1,), (1,)), ((), ()))


def _silu(x):
    half = 0.5 * x
    return half * jnp.tanh(half) + half


def _softplus(x):
    return jnp.maximum(x, 0.0) + jnp.log1p(jnp.exp(-jnp.abs(x)))


def _rms(x, w):
    ms = jnp.mean(x * x, axis=-1, keepdims=True)
    return x * lax.rsqrt(ms + NORM_EPS) * w


def _attn_scores(q_scr, k_blk_fn):
    return [jnp.dot(k_blk_fn(hd), q_scr[hd], preferred_element_type=F32)
            for hd in range(ATTN_HEADS)]


def _attn_softmax(s2s, maxes, mask, first, T):
    new, ps, alphas = [], [], []
    for hd in range(ATTN_HEADS):
        for mp in range(2):
            s = s2s[hd][:, mp * T:(mp + 1) * T]
            if mask is not None:
                s = jnp.where(mask, s, MASK_VALUE)
            m_new = jnp.max(s, axis=0, keepdims=True)
            if not first:
                m_old = maxes[2 * hd + mp]
                m_new = jnp.maximum(m_old, m_new)
                alphas.append(jnp.exp2(m_old - m_new))
            ps.append(jnp.exp2(s - m_new).astype(BF16))
            new.append(m_new)
    return tuple(new), ps, alphas


def _attn_accumulate(acc_scr, ps, alphas, vT_blk_fn, first):
    for hd in range(ATTN_HEADS):
        pv = jnp.dot(vT_blk_fn(hd), jnp.concatenate(ps[2 * hd:2 * hd + 2], axis=1),
                     preferred_element_type=F32)
        if first:
            acc_scr[hd] = pv
        else:
            acc_scr[hd] = jnp.concatenate(alphas[2 * hd:2 * hd + 2], axis=1) * acc_scr[hd] + pv


def _layer_kernel(h_ref, nw_ref, win_ref, wdt_ref, wdtT_ref, wout_ref, lam_ref, anw_ref,
                  scw_ref, xw_ref, xb_ref, alog_row_ref, alog_col_ref, dtb_row_ref, dtb_col_ref,
                  drep_ref, snw_ref, fnw_ref, kpre_ref, vpreT_ref, tsc_ref, tx_ref, s0_ref,
                  *rest, T, n_pad, has_prefix, emit_state, final_norm, lam_init):
    if emit_state:
        (hout_ref, kout_ref, vTout_ref, tsc_out_ref, tx_out_ref, s_out_ref,
         k_scr, vT_scr, q_scr, acc_scr, s2_scr, ext_sc, ext_x, s_scr) = rest
    else:
        hout_ref, k_scr, vT_scr, q_scr, acc_scr, s2_scr, ext_sc, ext_x, s_scr = rest

    i = pl.program_id(1)
    Q = CHUNK

    @pl.when(i == 0)
    def _init():
        ext_sc[0:TAIL, :] = tsc_ref[...]
        ext_x[0:TAIL, :] = tx_ref[...]
        s_scr[...] = s0_ref[...]

    if n_pad:
        row_ok = lax.broadcasted_iota(jnp.int32, (T, 1), 0) >= n_pad
        col_ok = lax.broadcasted_iota(jnp.int32, (1, T), 1) >= n_pad

    h = h_ref[0]
    u = _rms(h, nw_ref[...])
    if n_pad:
        u = jnp.where(row_ok, u, 0.0)
    ub = u.astype(BF16)

    def proj(off, width):
        return jnp.dot(ub, win_ref[:, off:off + width], preferred_element_type=F32)

    def out_proj(y, off):
        return jnp.dot(y.astype(BF16), wout_ref[off:off + y.shape[1], :], preferred_element_type=F32)

    def causal_conv(raw, ext_ref, cols, w_ref, width, bias):
        ext_ref[TAIL:TAIL + T, cols] = raw
        acc = raw * w_ref[width - 1:width, cols]
        if bias is not None:
            acc = acc + bias
        for j in range(width - 1):
            shift = width - 1 - j
            acc = acc + ext_ref[TAIL - shift:TAIL - shift + T, cols] * w_ref[j:j + 1, cols]
        ext_ref[0:TAIL, cols] = ext_ref[T:T + TAIL, cols]
        return acc

    def head_cols(hd):
        return slice(hd * ATTN_V_DIM, (hd + 1) * ATTN_V_DIM)

    qkv = proj(OFF_QKV, 3 * D_ATTN)
    sc = proj(OFF_SC, 4 * D_CONV)
    kb = qkv[:, D_ATTN:2 * D_ATTN].astype(BF16)
    vT = qkv[:, 2 * D_ATTN:3 * D_ATTN].T
    ones_rows = (lax.broadcasted_iota(jnp.int32, (V_ROWS - ATTN_V_DIM, T), 0) == 0).astype(BF16)
    vTb = [jnp.concatenate([vT[hd * ATTN_V_DIM:(hd + 1) * ATTN_V_DIM].astype(BF16), ones_rows], axis=0)
           for hd in range(ATTN_HEADS)]
    k_scr[pl.ds(pl.multiple_of(i * T, CHUNK), T), :] = kb
    for hd in range(ATTN_HEADS):
        vT_scr[i, hd] = vTb[hd]

    q = qkv[:, 0:D_ATTN] * (ATTN_QK_DIM ** -0.5 * LOG2_E)
    lane = lax.broadcasted_iota(jnp.int32, (T, ATTN_V_DIM), 1)
    for hd in range(ATTN_HEADS):
        qh = q[:, head_cols(hd)]
        q_scr[hd, :, 0:T] = jnp.where(lane < ATTN_QK_DIM, qh, 0.0).T.astype(BF16)
        q_scr[hd, :, T:2 * T] = jnp.where(lane >= ATTN_QK_DIM, qh, 0.0).T.astype(BF16)

    cch = sc[:, D_CONV:2 * D_CONV] * sc[:, 2 * D_CONV:3 * D_CONV]
    conv = causal_conv(cch, ext_sc, slice(0, D_CONV), scw_ref, SCONV_WIDTH, None)
    y_conv = sc[:, 0:D_CONV] * conv * _silu(sc[:, 3 * D_CONV:4 * D_CONV])

    def tile_keys(j):
        off = pl.multiple_of(j * T, CHUNK)
        return lambda hd: k_scr[pl.ds(off, T), head_cols(hd)]

    def stash_scores(s2s):
        for hd in range(ATTN_HEADS):
            s2_scr[hd] = s2s[hd]

    maxes = None
    if has_prefix:
        prefix_mask = lax.broadcasted_iota(jnp.int32, (CHUNK, T), 0) >= N_PAD
        s2p = _attn_scores(q_scr, lambda hd: kpre_ref[:, head_cols(hd)])
        maxes, ps, alphas = _attn_softmax(s2p, None, prefix_mask, True, T)
        s2n = _attn_scores(q_scr, tile_keys(0))
        _attn_accumulate(acc_scr, ps, alphas, lambda hd: vpreT_ref[hd], True)
        stash_scores(s2n)

        def body(j, mx):
            new, ps, alphas = _attn_softmax([s2_scr[hd] for hd in range(ATTN_HEADS)], mx, None, False, T)
            s2n = _attn_scores(q_scr, tile_keys(j + 1))
            _attn_accumulate(acc_scr, ps, alphas, lambda hd: vT_scr[j, hd], False)
            stash_scores(s2n)
            return new

        maxes = lax.fori_loop(0, i, body, maxes)
        s2d = [s2_scr[hd] for hd in range(ATTN_HEADS)]
    else:
        s2d = _attn_scores(q_scr, lambda hd: kb[:, head_cols(hd)])

    key_i = lax.broadcasted_iota(jnp.int32, (T, T), 0)
    qry_i = lax.broadcasted_iota(jnp.int32, (T, T), 1)
    diag_mask = key_i <= qry_i
    if n_pad:
        diag_mask = jnp.logical_and(diag_mask, key_i >= n_pad)
    _, ps, alphas = _attn_softmax(s2d, maxes, diag_mask, not has_prefix, T)
    xs_raw = proj(OFF_XBC, D_SSM)
    _attn_accumulate(acc_scr, ps, alphas, lambda hd: vTb[hd], not has_prefix)
    bc_raw = proj(OFF_XBC + D_SSM, SSM_CONV_DIM - D_SSM)
    xs = _silu(causal_conv(xs_raw, ext_x, slice(0, D_SSM), xw_ref, SSM_CONV_WIDTH, xb_ref[:, 0:D_SSM]))
    gate_a = proj(OFF_GATE_A, D_ATTN)
    z = proj(OFF_Z, D_SSM)
    bc = _silu(causal_conv(bc_raw, ext_x, slice(D_SSM, SSM_CONV_DIM), xw_ref, SSM_CONV_WIDTH,
                           xb_ref[:, D_SSM:SSM_CONV_DIM]))
    if n_pad:
        xs = jnp.where(row_ok, xs, 0.0)
        bc = jnp.where(row_ok, bc, 0.0)

    lp = lam_ref[...]
    lam = (jnp.exp(jnp.sum(lp[0:1] * lp[1:2], axis=-1, keepdims=True))
           - jnp.exp(jnp.sum(lp[2:3] * lp[3:4], axis=-1, keepdims=True)) + lam_init)
    y_attn = []
    for hd in range(ATTN_HEADS):
        dv = ATTN_V_DIM
        inv_l0 = 1.0 / acc_scr[hd, dv:dv + 1, 0:T]
        inv_l1 = 1.0 / acc_scr[hd, dv:dv + 1, T:2 * T]
        oT = acc_scr[hd, 0:dv, 0:T] * inv_l0 - lam * (acc_scr[hd, 0:dv, T:2 * T] * inv_l1)
        ms = jnp.mean(oT * oT, axis=0, keepdims=True)
        oT = oT * lax.rsqrt(ms + NORM_EPS) * anw_ref[...] * (1.0 - lam_init)
        y_attn.append(oT.T * _silu(gate_a[:, head_cols(hd)]))
    out = h_ref[0] + out_proj(y_conv, D_ATTN)
    out = out + out_proj(jnp.concatenate(y_attn, axis=1), 0)

    dt = _softplus(jnp.dot(ub, wdt_ref[...], preferred_element_type=F32) + dtb_row_ref[...])
    dtT = _softplus(lax.dot_general(wdtT_ref[...], ub, NT_DIMS, preferred_element_type=F32)
                    + dtb_col_ref[...])
    if n_pad:
        dt = jnp.where(row_ok, dt, 0.0)
        dtT = jnp.where(col_ok, dtT, 0.0)
    adt = dt * (-jnp.exp(alog_row_ref[...]))
    adtT = dtT * (-jnp.exp(alog_col_ref[...]))

    qr = lax.broadcasted_iota(jnp.int32, (Q, Q), 0)
    qc = lax.broadcasted_iota(jnp.int32, (Q, Q), 1)
    tril = qr >= qc
    lower_ones = tril.astype(F32)
    upper_ones = (qr <= qc).astype(F32)
    expand = (lax.broadcasted_iota(jnp.int32, (CHUNK, D_SSM), 0)
              == (lax.broadcasted_iota(jnp.int32, (CHUNK, D_SSM), 1) // SSM_HEAD_DIM)).astype(F32)
    colblk = lax.broadcasted_iota(jnp.int32, (Q, GROUP_W), 1) // SSM_HEAD_DIM
    head_mask = [jnp.where(colblk == hh, 1.0, 0.0).astype(BF16) for hh in range(HEADS_PER_GROUP)]
    low_half = lax.broadcasted_iota(jnp.int32, (Q, CHUNK), 1) < SSM_HEAD_DIM

    out_ssm = []
    for c in range(T // Q):
        rows = slice(c * Q, (c + 1) * Q)
        acum = jnp.dot(lower_ones, adt[rows], preferred_element_type=F32,
                       precision=lax.Precision.HIGHEST)
        acumT = jnp.dot(adtT[:, rows], upper_ones, preferred_element_type=F32,
                        precision=lax.Precision.HIGHEST)
        dtT_c = dtT[:, rows]
        wT = jnp.exp(acumT[:, Q - 1:Q] - acumT) * dtT_c
        total8 = jnp.broadcast_to(acum[Q - 1:Q, :], (8, CHUNK))
        chunk_decay = jnp.exp(jnp.dot(total8, expand, preferred_element_type=F32,
                                      precision=lax.Precision.HIGHEST))[0:1, :]
        xs_c = xs[rows]
        b_gs = [bc[rows, g * SSM_STATE:(g + 1) * SSM_STATE] for g in range(SSM_GROUPS)]
        c_gs = [bc[rows, (SSM_GROUPS + g) * SSM_STATE:(SSM_GROUPS + g + 1) * SSM_STATE].astype(BF16)
                for g in range(SSM_GROUPS)]
        s_prevs = [s_scr[g] for g in range(SSM_GROUPS)]
        cbs = [lax.dot_general(c_gs[g], b_gs[g].astype(BF16), NT_DIMS, preferred_element_type=F32)
               for g in range(SSM_GROUPS)]
        y_offs = [jnp.dot(c_gs[g], s_prevs[g].astype(BF16), preferred_element_type=F32)
                  for g in range(SSM_GROUPS)]
        lhs_diag, lhs_state, x_bds, e_gs = [], [], [], []
        for g in range(SSM_GROUPS):
            gcols = slice(g * GROUP_W, (g + 1) * GROUP_W)
            b_gT = b_gs[g].T
            xgb = xs_c[:, gcols].astype(BF16)
            x_bds.append(jnp.concatenate([xgb * head_mask[hh] for hh in range(HEADS_PER_GROUP)], axis=0))
            ms, bws, eacs = [], [], []
            for hh in range(HEADS_PER_GROUP):
                hidx = g * HEADS_PER_GROUP + hh
                acol = jnp.broadcast_to(acum[:, hidx:hidx + 1], (Q, Q))
                arow = acumT[hidx:hidx + 1, :]
                decay = jnp.exp(jnp.where(tril, acol - arow, MASK_VALUE))
                ms.append((cbs[g] * decay * dtT_c[hidx:hidx + 1, :]).astype(BF16))
                bws.append((b_gT * wT[hidx:hidx + 1, :]).astype(BF16))
                eacs.append(jnp.exp(acol))
            e_gs.append(jnp.concatenate([jnp.where(low_half, eacs[0], eacs[1]),
                                         jnp.where(low_half, eacs[2], eacs[3])], axis=1))
            lhs_diag.append(jnp.concatenate(ms, axis=1))
            lhs_state.append(jnp.concatenate(bws, axis=1))
        y_diags = [jnp.dot(lhs_diag[g], x_bds[g], preferred_element_type=F32) for g in range(SSM_GROUPS)]
        s_incs = [jnp.dot(lhs_state[g], x_bds[g], preferred_element_type=F32) for g in range(SSM_GROUPS)]
        ys = []
        for g in range(SSM_GROUPS):
            gcols = slice(g * GROUP_W, (g + 1) * GROUP_W)
            ys.append(y_diags[g] + y_offs[g] * e_gs[g])
            s_scr[g] = chunk_decay[:, gcols] * s_prevs[g] + s_incs[g]
        y_c = jnp.concatenate(ys, axis=1) + xs_c * drep_ref[...]
        ug = y_c * _silu(z[rows])
        parts = []
        for g in range(SSM_GROUPS):
            blk = ug[:, g * GROUP_W:(g + 1) * GROUP_W]
            ms_g = jnp.mean(blk * blk, axis=-1, keepdims=True)
            parts.append(blk * lax.rsqrt(ms_g + NORM_EPS))
        yn = jnp.concatenate(parts, axis=1) * snw_ref[...]
        out_ssm.append(out_proj(yn, D_ATTN + D_CONV))

    out = out + jnp.concatenate(out_ssm, axis=0)
    if final_norm:
        out = _rms(out, fnw_ref[...])
    hout_ref[0] = out

    if emit_state:
        kout_ref[...] = kb
        for hd in range(ATTN_HEADS):
            vTout_ref[hd] = vTb[hd]
        tsc_out_ref[...] = ext_sc[0:TAIL, :]
        tx_out_ref[...] = ext_x[0:TAIL, :]
        s_out_ref[...] = s_scr[...]


def _const_spec(shape):
    nd = len(shape)
    return pl.BlockSpec(shape, lambda b, i, _nd=nd: (0,) * _nd, pipeline_mode=pl.Buffered(1))


def _layer_call(h, params, state, *, T, n_pad, has_prefix, emit_state, final_norm, lam_init):
    bsz, seq, _ = h.shape
    nt = seq // T
    kern = functools.partial(_layer_kernel, T=T, n_pad=n_pad, has_prefix=has_prefix,
                             emit_state=emit_state, final_norm=final_norm, lam_init=lam_init)
    ins = [h] + list(params) + list(state)
    in_specs = [pl.BlockSpec((1, T, D_MODEL), lambda b, i: (b, i, 0))]
    in_specs += [_const_spec(a.shape) for a in list(params) + list(state)]
    out_shape = [jax.ShapeDtypeStruct((bsz, seq, D_MODEL), F32)]
    out_specs = [pl.BlockSpec((1, T, D_MODEL), lambda b, i: (b, i, 0))]
    if emit_state:
        st_shapes = [((T, D_ATTN), BF16), ((ATTN_HEADS, V_ROWS, T), BF16), ((TAIL, D_CONV), F32),
                     ((TAIL, SSM_CONV_DIM), F32), ((SSM_GROUPS, SSM_STATE, GROUP_W), F32)]
        for shp, dt_ in st_shapes:
            out_shape.append(jax.ShapeDtypeStruct(shp, dt_))
            out_specs.append(pl.BlockSpec(shp, lambda b, i, _nd=len(shp): (0,) * _nd))
    scratch = [
        pltpu.VMEM((seq, D_ATTN), BF16),
        pltpu.VMEM((nt, ATTN_HEADS, V_ROWS, T), BF16),
        pltpu.VMEM((ATTN_HEADS, ATTN_V_DIM, 2 * T), BF16),
        pltpu.VMEM((ATTN_HEADS, V_ROWS, 2 * T), F32),
        pltpu.VMEM((ATTN_HEADS, T, 2 * T), F32),
        pltpu.VMEM((TAIL + T, D_CONV), F32),
        pltpu.VMEM((TAIL + T, SSM_CONV_DIM), F32),
        pltpu.VMEM((SSM_GROUPS, SSM_STATE, GROUP_W), F32),
    ]
    return pl.pallas_call(
        kern,
        grid=(bsz, nt),
        in_specs=in_specs,
        out_specs=out_specs,
        out_shape=out_shape,
        scratch_shapes=scratch,
        compiler_params=pltpu.CompilerParams(
            dimension_semantics=("arbitrary", "arbitrary"),
            vmem_limit_bytes=VMEM_LIMIT_BYTES),
        name="layer_meta" if emit_state else "layer_main",
    )(*ins)


def kernel(x, meta_tokens, norm_w, w_in, w_out, attn_lambda, attn_norm_w, sconv_w, ssm_conv_w,
           ssm_conv_b, ssm_a_log, ssm_dt_bias, ssm_d, ssm_norm_w, final_norm_w):
    pad_heads = CHUNK - SSM_HEADS
    h_meta = jnp.concatenate([jnp.zeros((N_PAD, D_MODEL), F32), meta_tokens.astype(F32)], axis=0)[None]
    h_main = x.astype(F32)
    zero_state = (
        jnp.zeros((CHUNK, D_ATTN), BF16), jnp.zeros((ATTN_HEADS, V_ROWS, CHUNK), BF16),
        jnp.zeros((TAIL, D_CONV), F32), jnp.zeros((TAIL, SSM_CONV_DIM), F32),
        jnp.zeros((SSM_GROUPS, SSM_STATE, GROUP_W), F32),
    )
    fnw = final_norm_w.astype(F32)[None, :]
    for layer in range(DEPTH):
        lam_init = 0.8 - 0.6 * math.exp(-0.3 * layer)
        w_l = w_in[layer]
        w_dt = w_l[:, OFF_DT:OFF_DT + SSM_HEADS]
        params = (
            norm_w[layer].astype(F32)[None, :],
            w_l.astype(BF16),
            jnp.pad(w_dt, ((0, 0), (0, pad_heads))).astype(BF16),
            w_dt.T.astype(BF16),
            w_out[layer].astype(BF16),
            attn_lambda[layer].astype(F32),
            attn_norm_w[layer].astype(F32)[:, None],
            sconv_w[layer].astype(F32),
            ssm_conv_w[layer].astype(F32),
            ssm_conv_b[layer].astype(F32)[None, :],
            jnp.pad(ssm_a_log[layer].astype(F32), (0, pad_heads), constant_values=-jnp.inf)[None, :],
            ssm_a_log[layer].astype(F32)[:, None],
            jnp.pad(ssm_dt_bias[layer].astype(F32), (0, pad_heads))[None, :],
            ssm_dt_bias[layer].astype(F32)[:, None],
            jnp.repeat(ssm_d[layer].astype(F32), SSM_HEAD_DIM)[None, :],
            ssm_norm_w[layer].astype(F32)[None, :],
            fnw,
        )
        meta_out = _layer_call(h_meta, params, zero_state, T=CHUNK, n_pad=N_PAD, has_prefix=False,
                               emit_state=True, final_norm=False, lam_init=lam_init)
        h_meta, meta_state = meta_out[0], tuple(meta_out[1:])
        (h_main,) = _layer_call(h_main, params, meta_state, T=MAIN_TILE, n_pad=0, has_prefix=True,
                                emit_state=False, final_norm=(layer == DEPTH - 1), lam_init=lam_init)
    return h_main.astype(x.dtype)
```

```python
import functools
import math

import jax
import jax.numpy as jnp
from jax import lax
from jax.experimental import pallas as pl
from jax.experimental.pallas import tpu as pltpu

D_MODEL = 1024
DEPTH = 4
N_META = 16
D_ATTN = 512
D_CONV = 512
D_SSM = 1024
ATTN_HEADS = 4
ATTN_QK_DIM = 64
ATTN_V_DIM = 128
SCONV_WIDTH = 3
SSM_HEAD_DIM = 64
SSM_HEADS = 16
SSM_GROUPS = 4
SSM_STATE = 128
SSM_CONV_WIDTH = 4
SSM_CONV_DIM = 2048
CHUNK = 128
N_PAD = CHUNK - N_META
NORM_EPS = 1e-5
MASK_VALUE = -1e30
LOG2_E = 1.4426950408889634
V_ROWS = ATTN_V_DIM + 16
GROUP_W = D_SSM // SSM_GROUPS
HEADS_PER_GROUP = SSM_HEADS // SSM_GROUPS

OFF_QKV = 0
OFF_GATE_A = 3 * D_ATTN
OFF_SC = 4 * D_ATTN
OFF_Z = OFF_SC + 4 * D_CONV
OFF_XBC = OFF_Z + D_SSM
OFF_DT = OFF_XBC + SSM_CONV_DIM

TAIL = 8
MAIN_TILE = 256
VMEM_LIMIT_BYTES = 56 * 1024 * 1024

F32 = jnp.float32
BF16 = jnp.bfloat16
NT_DIMS = (((1,), (1,)), ((), ()))


def _silu(x):
    half = 0.5 * x
    return half * jnp.tanh(half) + half


def _softplus(x):
    return jnp.maximum(x, 0.0) + jnp.log1p(jnp.exp(-jnp.abs(x)))


def _rms(x, w):
    ms = jnp.mean(x * x, axis=-1, keepdims=True)
    return x * lax.rsqrt(ms + NORM_EPS) * w


def _attn_scores(q_scr, k_blk_fn):
    return [jnp.dot(k_blk_fn(hd), q_scr[hd], preferred_element_type=F32)
            for hd in range(ATTN_HEADS)]


def _attn_softmax(s2s, maxes, mask, first, T):
    new, ps, alphas = [], [], []
    for hd in range(ATTN_HEADS):
        for mp in range(2):
            s = s2s[hd][:, mp * T:(mp + 1) * T]
            if mask is not None:
                s = jnp.where(mask, s, MASK_VALUE)
            m_new = jnp.max(s, axis=0, keepdims=True)
            if not first:
                m_old = maxes[2 * hd + mp]
                m_new = jnp.maximum(m_old, m_new)
                alphas.append(jnp.exp2(m_old - m_new))
            ps.append(jnp.exp2(s - m_new).astype(BF16))
            new.append(m_new)
    return tuple(new), ps, alphas


def _attn_accumulate(acc_scr, ps, alphas, vT_blk_fn, first, zero_rows=0):
    for hd in range(ATTN_HEADS):
        pcat = jnp.concatenate(ps[2 * hd:2 * hd + 2], axis=1)
        if zero_rows:
            pcat = jnp.concatenate([jnp.zeros((zero_rows, pcat.shape[1]), BF16), pcat], axis=0)
        pv = jnp.dot(vT_blk_fn(hd), pcat, preferred_element_type=F32)
        if first:
            acc_scr[hd] = pv
        else:
            acc_scr[hd] = jnp.concatenate(alphas[2 * hd:2 * hd + 2], axis=1) * acc_scr[hd] + pv


def _layer_kernel(h_ref, nw_ref, win_ref, wdt_ref, wdtT_ref, wout_ref, lam_ref, anw_ref,
                  scw_ref, xw_ref, xb_ref, alog_row_ref, alog_col_ref, dtb_row_ref, dtb_col_ref,
                  drep_ref, snw_ref, fnw_ref, kpre_ref, vpreT_ref, tsc_ref, tx_ref, s0_ref,
                  *rest, T, n_pad, has_prefix, emit_state, final_norm, lam_init):
    if emit_state:
        (hout_ref, kout_ref, vTout_ref, tsc_out_ref, tx_out_ref, s_out_ref,
         k_scr, vT_scr, q_scr, acc_scr, s2_scr, ext_sc, ext_x, s_scr) = rest
    else:
        hout_ref, k_scr, vT_scr, q_scr, acc_scr, s2_scr, ext_sc, ext_x, s_scr = rest

    i = pl.program_id(1)
    Q = CHUNK

    @pl.when(i == 0)
    def _init():
        ext_sc[0:TAIL, :] = tsc_ref[...]
        ext_x[0:TAIL, :] = tx_ref[...]
        s_scr[...] = s0_ref[...]

    if n_pad:
        row_ok = lax.broadcasted_iota(jnp.int32, (T, 1), 0) >= n_pad
        col_ok = lax.broadcasted_iota(jnp.int32, (1, T), 1) >= n_pad

    h = h_ref[0]
    u = _rms(h, nw_ref[...])
    if n_pad:
        u = jnp.where(row_ok, u, 0.0)
    ub = u.astype(BF16)

    def proj(off, width):
        return jnp.dot(ub, win_ref[:, off:off + width], preferred_element_type=F32)

    def out_proj(y, off):
        return jnp.dot(y.astype(BF16), wout_ref[off:off + y.shape[1], :], preferred_element_type=F32)

    def causal_conv(raw, ext_ref, cols, w_ref, width, bias):
        ext_ref[TAIL:TAIL + T, cols] = raw
        acc = raw * w_ref[width - 1:width, cols]
        if bias is not None:
            acc = acc + bias
        for j in range(width - 1):
            shift = width - 1 - j
            acc = acc + ext_ref[TAIL - shift:TAIL - shift + T, cols] * w_ref[j:j + 1, cols]
        ext_ref[0:TAIL, cols] = ext_ref[T:T + TAIL, cols]
        return acc

    def head_cols(hd):
        return slice(hd * ATTN_V_DIM, (hd + 1) * ATTN_V_DIM)

    qkv = proj(OFF_QKV, 3 * D_ATTN)
    sc = proj(OFF_SC, 4 * D_CONV)
    kb = qkv[:, D_ATTN:2 * D_ATTN].astype(BF16)
    vT = qkv[:, 2 * D_ATTN:3 * D_ATTN].T
    ones_rows = (lax.broadcasted_iota(jnp.int32, (V_ROWS - ATTN_V_DIM, T), 0) == 0).astype(BF16)
    vTb = [jnp.concatenate([vT[hd * ATTN_V_DIM:(hd + 1) * ATTN_V_DIM].astype(BF16), ones_rows], axis=0)
           for hd in range(ATTN_HEADS)]
    k_scr[pl.ds(pl.multiple_of(i * T, CHUNK), T), :] = kb
    for hd in range(ATTN_HEADS):
        vT_scr[i, hd] = vTb[hd]

    q = qkv[:, 0:D_ATTN] * (ATTN_QK_DIM ** -0.5 * LOG2_E)
    lane = lax.broadcasted_iota(jnp.int32, (T, ATTN_V_DIM), 1)
    for hd in range(ATTN_HEADS):
        qh = q[:, head_cols(hd)]
        q_scr[hd, :, 0:T] = jnp.where(lane < ATTN_QK_DIM, qh, 0.0).T.astype(BF16)
        q_scr[hd, :, T:2 * T] = jnp.where(lane >= ATTN_QK_DIM, qh, 0.0).T.astype(BF16)

    cch = sc[:, D_CONV:2 * D_CONV] * sc[:, 2 * D_CONV:3 * D_CONV]
    conv = causal_conv(cch, ext_sc, slice(0, D_CONV), scw_ref, SCONV_WIDTH, None)
    y_conv = sc[:, 0:D_CONV] * conv * _silu(sc[:, 3 * D_CONV:4 * D_CONV])

    def tile_keys(j):
        off = pl.multiple_of(j * T, CHUNK)
        return lambda hd: k_scr[pl.ds(off, T), head_cols(hd)]

    def stash_scores(s2s):
        for hd in range(ATTN_HEADS):
            s2_scr[hd] = s2s[hd]

    maxes = None
    if has_prefix:
        s2p = _attn_scores(q_scr, lambda hd: kpre_ref[N_PAD:CHUNK, head_cols(hd)])
        maxes, ps, alphas = _attn_softmax(s2p, None, None, True, T)
        s2n = _attn_scores(q_scr, tile_keys(0))
        _attn_accumulate(acc_scr, ps, alphas, lambda hd: vpreT_ref[hd], True, zero_rows=N_PAD)
        stash_scores(s2n)

        def body(j, mx):
            new, ps, alphas = _attn_softmax([s2_scr[hd] for hd in range(ATTN_HEADS)], mx, None, False, T)
            s2n = _attn_scores(q_scr, tile_keys(j + 1))
            _attn_accumulate(acc_scr, ps, alphas, lambda hd: vT_scr[j, hd], False)
            stash_scores(s2n)
            return new

        maxes = lax.fori_loop(0, i, body, maxes)
        s2d = [s2_scr[hd] for hd in range(ATTN_HEADS)]
    else:
        s2d = _attn_scores(q_scr, lambda hd: kb[:, head_cols(hd)])

    key_i = lax.broadcasted_iota(jnp.int32, (T, T), 0)
    qry_i = lax.broadcasted_iota(jnp.int32, (T, T), 1)
    diag_mask = key_i <= qry_i
    if n_pad:
        diag_mask = jnp.logical_and(diag_mask, key_i >= n_pad)
    _, ps, alphas = _attn_softmax(s2d, maxes, diag_mask, not has_prefix, T)
    xs_raw = proj(OFF_XBC, D_SSM)
    _attn_accumulate(acc_scr, ps, alphas, lambda hd: vTb[hd], not has_prefix)
    bc_raw = proj(OFF_XBC + D_SSM, SSM_CONV_DIM - D_SSM)
    xs = _silu(causal_conv(xs_raw, ext_x, slice(0, D_SSM), xw_ref, SSM_CONV_WIDTH, xb_ref[:, 0:D_SSM]))
    gate_a = proj(OFF_GATE_A, D_ATTN)
    z = proj(OFF_Z, D_SSM)
    bc = _silu(causal_conv(bc_raw, ext_x, slice(D_SSM, SSM_CONV_DIM), xw_ref, SSM_CONV_WIDTH,
                           xb_ref[:, D_SSM:SSM_CONV_DIM]))
    if n_pad:
        xs = jnp.where(row_ok, xs, 0.0)
        bc = jnp.where(row_ok, bc, 0.0)

    lp = lam_ref[...]
    lam = (jnp.exp(jnp.sum(lp[0:1] * lp[1:2], axis=-1, keepdims=True))
           - jnp.exp(jnp.sum(lp[2:3] * lp[3:4], axis=-1, keepdims=True)) + lam_init)
    y_attn = []
    for hd in range(ATTN_HEADS):
        dv = ATTN_V_DIM
        inv_l0 = 1.0 / acc_scr[hd, dv:dv + 1, 0:T]
        inv_l1 = 1.0 / acc_scr[hd, dv:dv + 1, T:2 * T]
        oT = acc_scr[hd, 0:dv, 0:T] * inv_l0 - lam * (acc_scr[hd, 0:dv, T:2 * T] * inv_l1)
        ms = jnp.mean(oT * oT, axis=0, keepdims=True)
        oT = oT * lax.rsqrt(ms + NORM_EPS) * anw_ref[...] * (1.0 - lam_init)
        y_attn.append(oT.T * _silu(gate_a[:, head_cols(hd)]))
    out = h_ref[0] + out_proj(y_conv, D_ATTN)
    out = out + out_proj(jnp.concatenate(y_attn, axis=1), 0)

    dt = _softplus(jnp.dot(ub, wdt_ref[...], preferred_element_type=F32) + dtb_row_ref[...])
    dtT = _softplus(lax.dot_general(wdtT_ref[...], ub, NT_DIMS, preferred_element_type=F32)
                    + dtb_col_ref[...])
    if n_pad:
        dt = jnp.where(row_ok, dt, 0.0)
        dtT = jnp.where(col_ok, dtT, 0.0)
    adt = dt * (-jnp.exp(alog_row_ref[...]))
    adtT = dtT * (-jnp.exp(alog_col_ref[...]))

    qr = lax.broadcasted_iota(jnp.int32, (Q, Q), 0)
    qc = lax.broadcasted_iota(jnp.int32, (Q, Q), 1)
    tril = qr >= qc
    lower_ones = tril.astype(F32)
    upper_ones = (qr <= qc).astype(F32)
    expand = (lax.broadcasted_iota(jnp.int32, (CHUNK, D_SSM), 0)
              == (lax.broadcasted_iota(jnp.int32, (CHUNK, D_SSM), 1) // SSM_HEAD_DIM)).astype(F32)
    colblk = lax.broadcasted_iota(jnp.int32, (Q, GROUP_W), 1) // SSM_HEAD_DIM
    head_mask = [jnp.where(colblk == hh, 1.0, 0.0).astype(BF16) for hh in range(HEADS_PER_GROUP)]
    low_half = lax.broadcasted_iota(jnp.int32, (Q, CHUNK), 1) < SSM_HEAD_DIM

    out_ssm = []
    for c in range(T // Q):
        rows = slice(c * Q, (c + 1) * Q)
        acum = jnp.dot(lower_ones, adt[rows], preferred_element_type=F32,
                       precision=lax.Precision.HIGHEST)
        acumT = jnp.dot(adtT[:, rows], upper_ones, preferred_element_type=F32,
                        precision=lax.Precision.HIGHEST)
        dtT_c = dtT[:, rows]
        wT = jnp.exp(acumT[:, Q - 1:Q] - acumT) * dtT_c
        total8 = jnp.broadcast_to(acum[Q - 1:Q, :], (8, CHUNK))
        chunk_decay = jnp.exp(jnp.dot(total8, expand, preferred_element_type=F32,
                                      precision=lax.Precision.HIGHEST))[0:1, :]
        xs_c = xs[rows]
        b_gs = [bc[rows, g * SSM_STATE:(g + 1) * SSM_STATE] for g in range(SSM_GROUPS)]
        c_gs = [bc[rows, (SSM_GROUPS + g) * SSM_STATE:(SSM_GROUPS + g + 1) * SSM_STATE].astype(BF16)
                for g in range(SSM_GROUPS)]
        s_prevs = [s_scr[g] for g in range(SSM_GROUPS)]
        cbs = [lax.dot_general(c_gs[g], b_gs[g].astype(BF16), NT_DIMS, preferred_element_type=F32)
               for g in range(SSM_GROUPS)]
        y_offs = [jnp.dot(c_gs[g], s_prevs[g].astype(BF16), preferred_element_type=F32)
                  for g in range(SSM_GROUPS)]
        lhs_diag, lhs_state, x_bds, e_gs = [], [], [], []
        for g in range(SSM_GROUPS):
            gcols = slice(g * GROUP_W, (g + 1) * GROUP_W)
            b_gT = b_gs[g].T
            xgb = xs_c[:, gcols].astype(BF16)
            x_bds.append(jnp.concatenate([xgb * head_mask[hh] for hh in range(HEADS_PER_GROUP)], axis=0))
            ms, bws, eacs = [], [], []
            for hh in range(HEADS_PER_GROUP):
                hidx = g * HEADS_PER_GROUP + hh
                acol = jnp.broadcast_to(acum[:, hidx:hidx + 1], (Q, Q))
                arow = acumT[hidx:hidx + 1, :]
                decay = jnp.exp(jnp.where(tril, acol - arow, MASK_VALUE))
                ms.append((cbs[g] * decay * dtT_c[hidx:hidx + 1, :]).astype(BF16))
                bws.append((b_gT * wT[hidx:hidx + 1, :]).astype(BF16))
                eacs.append(jnp.exp(acol))
            e_gs.append(jnp.concatenate([jnp.where(low_half, eacs[0], eacs[1]),
                                         jnp.where(low_half, eacs[2], eacs[3])], axis=1))
            lhs_diag.append(jnp.concatenate(ms, axis=1))
            lhs_state.append(jnp.concatenate(bws, axis=1))
        y_diags = [jnp.dot(lhs_diag[g], x_bds[g], preferred_element_type=F32) for g in range(SSM_GROUPS)]
        s_incs = [jnp.dot(lhs_state[g], x_bds[g], preferred_element_type=F32) for g in range(SSM_GROUPS)]
        ys = []
        for g in range(SSM_GROUPS):
            gcols = slice(g * GROUP_W, (g + 1) * GROUP_W)
            ys.append(y_diags[g] + y_offs[g] * e_gs[g])
            s_scr[g] = chunk_decay[:, gcols] * s_prevs[g] + s_incs[g]
        y_c = jnp.concatenate(ys, axis=1) + xs_c * drep_ref[...]
        ug = y_c * _silu(z[rows])
        parts = []
        for g in range(SSM_GROUPS):
            blk = ug[:, g * GROUP_W:(g + 1) * GROUP_W]
            ms_g = jnp.mean(blk * blk, axis=-1, keepdims=True)
            parts.append(blk * lax.rsqrt(ms_g + NORM_EPS))
        yn = jnp.concatenate(parts, axis=1) * snw_ref[...]
        out_ssm.append(out_proj(yn, D_ATTN + D_CONV))

    out = out + jnp.concatenate(out_ssm, axis=0)
    if final_norm:
        out = _rms(out, fnw_ref[...])
    hout_ref[0] = out

    if emit_state:
        kout_ref[...] = kb
        for hd in range(ATTN_HEADS):
            vTout_ref[hd] = vTb[hd]
        tsc_out_ref[...] = ext_sc[0:TAIL, :]
        tx_out_ref[...] = ext_x[0:TAIL, :]
        s_out_ref[...] = s_scr[...]


def _const_spec(shape):
    nd = len(shape)
    return pl.BlockSpec(shape, lambda b, i, _nd=nd: (0,) * _nd, pipeline_mode=pl.Buffered(1))


def _layer_call(h, params, state, *, T, n_pad, has_prefix, emit_state, final_norm, lam_init):
    bsz, seq, _ = h.shape
    nt = seq // T
    kern = functools.partial(_layer_kernel, T=T, n_pad=n_pad, has_prefix=has_prefix,
                             emit_state=emit_state, final_norm=final_norm, lam_init=lam_init)
    ins = [h] + list(params) + list(state)
    in_specs = [pl.BlockSpec((1, T, D_MODEL), lambda b, i: (b, i, 0))]
    in_specs += [_const_spec(a.shape) for a in list(params) + list(state)]
    out_shape = [jax.ShapeDtypeStruct((bsz, seq, D_MODEL), F32)]
    out_specs = [pl.BlockSpec((1, T, D_MODEL), lambda b, i: (b, i, 0))]
    if emit_state:
        st_shapes = [((T, D_ATTN), BF16), ((ATTN_HEADS, V_ROWS, T), BF16), ((TAIL, D_CONV), F32),
                     ((TAIL, SSM_CONV_DIM), F32), ((SSM_GROUPS, SSM_STATE, GROUP_W), F32)]
        for shp, dt_ in st_shapes:
            out_shape.append(jax.ShapeDtypeStruct(shp, dt_))
            out_specs.append(pl.BlockSpec(shp, lambda b, i, _nd=len(shp): (0,) * _nd))
    scratch = [
        pltpu.VMEM((seq, D_ATTN), BF16),
        pltpu.VMEM((nt, ATTN_HEADS, V_ROWS, T), BF16),
        pltpu.VMEM((ATTN_HEADS, ATTN_V_DIM, 2 * T), BF16),
        pltpu.VMEM((ATTN_HEADS, V_ROWS, 2 * T), F32),
        pltpu.VMEM((ATTN_HEADS, T, 2 * T), F32),
        pltpu.VMEM((TAIL + T, D_CONV), F32),
        pltpu.VMEM((TAIL + T, SSM_CONV_DIM), F32),
        pltpu.VMEM((SSM_GROUPS, SSM_STATE, GROUP_W), F32),
    ]
    return pl.pallas_call(
        kern,
        grid=(bsz, nt),
        in_specs=in_specs,
        out_specs=out_specs,
        out_shape=out_shape,
        scratch_shapes=scratch,
        compiler_params=pltpu.CompilerParams(
            dimension_semantics=("arbitrary", "arbitrary"),
            vmem_limit_bytes=VMEM_LIMIT_BYTES),
        name="layer_meta" if emit_state else "layer_main",
    )(*ins)


def kernel(x, meta_tokens, norm_w, w_in, w_out, attn_lambda, attn_norm_w, sconv_w, ssm_conv_w,
           ssm_conv_b, ssm_a_log, ssm_dt_bias, ssm_d, ssm_norm_w, final_norm_w):
    pad_heads = CHUNK - SSM_HEADS
    h_meta = jnp.concatenate([jnp.zeros((N_PAD, D_MODEL), F32), meta_tokens.astype(F32)], axis=0)[None]
    h_main = x.astype(F32)
    zero_state = (
        jnp.zeros((CHUNK, D_ATTN), BF16), jnp.zeros((ATTN_HEADS, V_ROWS, CHUNK), BF16),
        jnp.zeros((TAIL, D_CONV), F32), jnp.zeros((TAIL, SSM_CONV_DIM), F32),
        jnp.zeros((SSM_GROUPS, SSM_STATE, GROUP_W), F32),
    )
    fnw = final_norm_w.astype(F32)[None, :]
    for layer in range(DEPTH):
        lam_init = 0.8 - 0.6 * math.exp(-0.3 * layer)
        w_l = w_in[layer]
        w_dt = w_l[:, OFF_DT:OFF_DT + SSM_HEADS]
        params = (
            norm_w[layer].astype(F32)[None, :],
            w_l.astype(BF16),
            jnp.pad(w_dt, ((0, 0), (0, pad_heads))).astype(BF16),
            w_dt.T.astype(BF16),
            w_out[layer].astype(BF16),
            attn_lambda[layer].astype(F32),
            attn_norm_w[layer].astype(F32)[:, None],
            sconv_w[layer].astype(F32),
            ssm_conv_w[layer].astype(F32),
            ssm_conv_b[layer].astype(F32)[None, :],
            jnp.pad(ssm_a_log[layer].astype(F32), (0, pad_heads), constant_values=-jnp.inf)[None, :],
            ssm_a_log[layer].astype(F32)[:, None],
            jnp.pad(ssm_dt_bias[layer].astype(F32), (0, pad_heads))[None, :],
            ssm_dt_bias[layer].astype(F32)[:, None],
            jnp.repeat(ssm_d[layer].astype(F32), SSM_HEAD_DIM)[None, :],
            ssm_norm_w[layer].astype(F32)[None, :],
            fnw,
        )
        meta_out = _layer_call(h_meta, params, zero_state, T=CHUNK, n_pad=N_PAD, has_prefix=False,
                               emit_state=True, final_norm=False, lam_init=lam_init)
        h_meta, meta_state = meta_out[0], tuple(meta_out[1:])
        (h_main,) = _layer_call(h_main, params, meta_state, T=MAIN_TILE, n_pad=0, has_prefix=True,
                                emit_state=False, final_norm=(layer == DEPTH - 1), lam_init=lam_init)
    return h_main.astype(x.dtype)
```

```python
import functools
import math

import jax
import jax.numpy as jnp
from jax import lax
from jax.experimental import pallas as pl
from jax.experimental.pallas import tpu as pltpu

D_MODEL = 1024
DEPTH = 4
N_META = 16
D_ATTN = 512
D_CONV = 512
D_SSM = 1024
ATTN_HEADS = 4
ATTN_QK_DIM = 64
ATTN_V_DIM = 128
SCONV_WIDTH = 3
SSM_HEAD_DIM = 64
SSM_HEADS = 16
SSM_GROUPS = 4
SSM_STATE = 128
SSM_CONV_WIDTH = 4
SSM_CONV_DIM = 2048
CHUNK = 128
N_PAD = CHUNK - N_META
NORM_EPS = 1e-5
MASK_VALUE = -1e30
LOG2_E = 1.4426950408889634
V_ROWS = ATTN_V_DIM + 16
GROUP_W = D_SSM // SSM_GROUPS
HEADS_PER_GROUP = SSM_HEADS // SSM_GROUPS

OFF_QKV = 0
OFF_GATE_A = 3 * D_ATTN
OFF_SC = 4 * D_ATTN
OFF_Z = OFF_SC + 4 * D_CONV
OFF_XBC = OFF_Z + D_SSM
OFF_DT = OFF_XBC + SSM_CONV_DIM

TAIL = 8
MAIN_TILE = 256
VMEM_LIMIT_BYTES = 56 * 1024 * 1024

F32 = jnp.float32
BF16 = jnp.bfloat16
NT_DIMS = (((1,), (1,)), ((), ()))


def _silu(x):
    half = 0.5 * x
    return half * jnp.tanh(half) + half


def _softplus(x):
    return jnp.maximum(x, 0.0) + jnp.log1p(jnp.exp(-jnp.abs(x)))


def _rms(x, w):
    ms = jnp.mean(x * x, axis=-1, keepdims=True)
    return x * lax.rsqrt(ms + NORM_EPS) * w


def _attn_scores(q_scr, k_blk_fn):
    return [jnp.dot(k_blk_fn(hd), q_scr[hd], preferred_element_type=F32)
            for hd in range(ATTN_HEADS)]


def _attn_softmax(s2s, maxes, mask, first, T):
    new, ps, alphas = [], [], []
    for hd in range(ATTN_HEADS):
        for mp in range(2):
            s = s2s[hd][:, mp * T:(mp + 1) * T]
            if mask is not None:
                s = jnp.where(mask, s, MASK_VALUE)
            m_new = jnp.max(s, axis=0, keepdims=True)
            if not first:
                m_old = maxes[2 * hd + mp]
                m_new = jnp.maximum(m_old, m_new)
                alphas.append(jnp.exp2(m_old - m_new))
            ps.append(jnp.exp2(s - m_new).astype(BF16))
            new.append(m_new)
    return tuple(new), ps, alphas


def _attn_accumulate(acc_scr, ps, alphas, vT_blk_fn, first):
    for hd in range(ATTN_HEADS):
        pv = jnp.dot(vT_blk_fn(hd), jnp.concatenate(ps[2 * hd:2 * hd + 2], axis=1),
                     preferred_element_type=F32)
        if first:
            acc_scr[hd] = pv
        else:
            acc_scr[hd] = jnp.concatenate(alphas[2 * hd:2 * hd + 2], axis=1) * acc_scr[hd] + pv


def _layer_kernel(h_ref, nw_ref, win_ref, wdt_ref, wdtT_ref, wout_ref, lam_ref, anw_ref,
                  scw_ref, xw_ref, xb_ref, alog_row_ref, alog_col_ref, dtb_row_ref, dtb_col_ref,
                  drep_ref, snw_ref, fnw_ref, kpre_ref, vpreT_ref, tsc_ref, tx_ref, s0_ref,
                  *rest, T, n_pad, has_prefix, emit_state, final_norm, lam_init):
    if emit_state:
        (hout_ref, kout_ref, vTout_ref, tsc_out_ref, tx_out_ref, s_out_ref,
         k_scr, vT_scr, q_scr, acc_scr, s2_scr, ext_sc, ext_x, s_scr) = rest
    else:
        hout_ref, k_scr, vT_scr, q_scr, acc_scr, s2_scr, ext_sc, ext_x, s_scr = rest

    i = pl.program_id(1)
    Q = CHUNK

    @pl.when(i == 0)
    def _init():
        ext_sc[0:TAIL, :] = tsc_ref[...]
        ext_x[0:TAIL, :] = tx_ref[...]
        s_scr[...] = s0_ref[...]

    if n_pad:
        row_ok = lax.broadcasted_iota(jnp.int32, (T, 1), 0) >= n_pad
        col_ok = lax.broadcasted_iota(jnp.int32, (1, T), 1) >= n_pad

    h = h_ref[0]
    u = _rms(h, nw_ref[...])
    if n_pad:
        u = jnp.where(row_ok, u, 0.0)
    ub = u.astype(BF16)

    def proj(off, width):
        return jnp.dot(ub, win_ref[:, off:off + width], preferred_element_type=F32)

    def out_proj(y, off):
        return jnp.dot(y.astype(BF16), wout_ref[off:off + y.shape[1], :], preferred_element_type=F32)

    def causal_conv(raw, ext_ref, cols, w_ref, width, bias):
        ext_ref[TAIL:TAIL + T, cols] = raw
        acc = raw * w_ref[width - 1:width, cols]
        if bias is not None:
            acc = acc + bias
        for j in range(width - 1):
            shift = width - 1 - j
            acc = acc + ext_ref[TAIL - shift:TAIL - shift + T, cols] * w_ref[j:j + 1, cols]
        ext_ref[0:TAIL, cols] = ext_ref[T:T + TAIL, cols]
        return acc

    def head_cols(hd):
        return slice(hd * ATTN_V_DIM, (hd + 1) * ATTN_V_DIM)

    qkv = proj(OFF_QKV, 3 * D_ATTN)
    sc = proj(OFF_SC, 4 * D_CONV)
    kb = qkv[:, D_ATTN:2 * D_ATTN].astype(BF16)
    vT = qkv[:, 2 * D_ATTN:3 * D_ATTN].T
    ones_rows = (lax.broadcasted_iota(jnp.int32, (V_ROWS - ATTN_V_DIM, T), 0) == 0).astype(BF16)
    vTb = [jnp.concatenate([vT[hd * ATTN_V_DIM:(hd + 1) * ATTN_V_DIM].astype(BF16), ones_rows], axis=0)
           for hd in range(ATTN_HEADS)]
    k_scr[pl.ds(pl.multiple_of(i * T, CHUNK), T), :] = kb
    for hd in range(ATTN_HEADS):
        vT_scr[i, hd] = vTb[hd]

    q = qkv[:, 0:D_ATTN] * (ATTN_QK_DIM ** -0.5 * LOG2_E)
    lane = lax.broadcasted_iota(jnp.int32, (T, ATTN_V_DIM), 1)
    for hd in range(ATTN_HEADS):
        qh = q[:, head_cols(hd)]
        q_scr[hd, :, 0:T] = jnp.where(lane < ATTN_QK_DIM, qh, 0.0).T.astype(BF16)
        q_scr[hd, :, T:2 * T] = jnp.where(lane >= ATTN_QK_DIM, qh, 0.0).T.astype(BF16)

    cch = sc[:, D_CONV:2 * D_CONV] * sc[:, 2 * D_CONV:3 * D_CONV]
    conv = causal_conv(cch, ext_sc, slice(0, D_CONV), scw_ref, SCONV_WIDTH, None)
    y_conv = sc[:, 0:D_CONV] * conv * _silu(sc[:, 3 * D_CONV:4 * D_CONV])

    def tile_keys(j):
        off = pl.multiple_of(j * T, CHUNK)
        return lambda hd: k_scr[pl.ds(off, T), head_cols(hd)]

    def stash_scores(s2s):
        for hd in range(ATTN_HEADS):
            s2_scr[hd] = s2s[hd]

    maxes = None
    if has_prefix:
        prefix_mask = lax.broadcasted_iota(jnp.int32, (CHUNK, T), 0) >= N_PAD
        s2p = _attn_scores(q_scr, lambda hd: kpre_ref[:, head_cols(hd)])
        maxes, ps, alphas = _attn_softmax(s2p, None, prefix_mask, True, T)
        s2n = _attn_scores(q_scr, tile_keys(0))
        _attn_accumulate(acc_scr, ps, alphas, lambda hd: vpreT_ref[hd], True)
        stash_scores(s2n)

        def body(j, mx):
            new, ps, alphas = _attn_softmax([s2_scr[hd] for hd in range(ATTN_HEADS)], mx, None, False, T)
            s2n = _attn_scores(q_scr, tile_keys(j + 1))
            _attn_accumulate(acc_scr, ps, alphas, lambda hd: vT_scr[j, hd], False)
            stash_scores(s2n)
            return new

        maxes = lax.fori_loop(0, i, body, maxes)
        s2d = [s2_scr[hd] for hd in range(ATTN_HEADS)]
    else:
        s2d = _attn_scores(q_scr, lambda hd: kb[:, head_cols(hd)])

    key_i = lax.broadcasted_iota(jnp.int32, (T, T), 0)
    qry_i = lax.broadcasted_iota(jnp.int32, (T, T), 1)
    diag_mask = key_i <= qry_i
    if n_pad:
        diag_mask = jnp.logical_and(diag_mask, key_i >= n_pad)
    _, ps, alphas = _attn_softmax(s2d, maxes, diag_mask, not has_prefix, T)
    xs_raw = proj(OFF_XBC, D_SSM)
    _attn_accumulate(acc_scr, ps, alphas, lambda hd: vTb[hd], not has_prefix)
    bc_raw = proj(OFF_XBC + D_SSM, SSM_CONV_DIM - D_SSM)
    xs = _silu(causal_conv(xs_raw, ext_x, slice(0, D_SSM), xw_ref, SSM_CONV_WIDTH, xb_ref[:, 0:D_SSM]))
    gate_a = proj(OFF_GATE_A, D_ATTN)
    z = proj(OFF_Z, D_SSM)
    bc = _silu(causal_conv(bc_raw, ext_x, slice(D_SSM, SSM_CONV_DIM), xw_ref, SSM_CONV_WIDTH,
                           xb_ref[:, D_SSM:SSM_CONV_DIM]))
    if n_pad:
        xs = jnp.where(row_ok, xs, 0.0)
        bc = jnp.where(row_ok, bc, 0.0)

    lp = lam_ref[...]
    lam = (jnp.exp(jnp.sum(lp[0:1] * lp[1:2], axis=-1, keepdims=True))
           - jnp.exp(jnp.sum(lp[2:3] * lp[3:4], axis=-1, keepdims=True)) + lam_init)
    y_attn = []
    for hd in range(ATTN_HEADS):
        dv = ATTN_V_DIM
        inv_l0 = 1.0 / acc_scr[hd, dv:dv + 1, 0:T]
        inv_l1 = 1.0 / acc_scr[hd, dv:dv + 1, T:2 * T]
        oT = acc_scr[hd, 0:dv, 0:T] * inv_l0 - lam * (acc_scr[hd, 0:dv, T:2 * T] * inv_l1)
        ms = jnp.mean(oT * oT, axis=0, keepdims=True)
        oT = oT * lax.rsqrt(ms + NORM_EPS) * anw_ref[...] * (1.0 - lam_init)
        y_attn.append(oT.T * _silu(gate_a[:, head_cols(hd)]))
    out = h_ref[0] + out_proj(y_conv, D_ATTN)
    out = out + out_proj(jnp.concatenate(y_attn, axis=1), 0)

    dt = _softplus(jnp.dot(ub, wdt_ref[...], preferred_element_type=F32) + dtb_row_ref[...])
    dtT = _softplus(lax.dot_general(wdtT_ref[...], ub, NT_DIMS, preferred_element_type=F32)
                    + dtb_col_ref[...])
    if n_pad:
        dt = jnp.where(row_ok, dt, 0.0)
        dtT = jnp.where(col_ok, dtT, 0.0)
    adt = dt * (-jnp.exp(alog_row_ref[...]))
    adtT = dtT * (-jnp.exp(alog_col_ref[...]))

    qr = lax.broadcasted_iota(jnp.int32, (Q, Q), 0)
    qc = lax.broadcasted_iota(jnp.int32, (Q, Q), 1)
    tril = qr >= qc
    lower_ones = tril.astype(F32)
    upper_ones = (qr <= qc).astype(F32)
    expand = (lax.broadcasted_iota(jnp.int32, (CHUNK, D_SSM), 0)
              == (lax.broadcasted_iota(jnp.int32, (CHUNK, D_SSM), 1) // SSM_HEAD_DIM)).astype(F32)
    colblk = lax.broadcasted_iota(jnp.int32, (Q, GROUP_W), 1) // SSM_HEAD_DIM
    head_mask = [jnp.where(colblk == hh, 1.0, 0.0).astype(BF16) for hh in range(HEADS_PER_GROUP)]
    low_half = lax.broadcasted_iota(jnp.int32, (Q, CHUNK), 1) < SSM_HEAD_DIM

    all_cbs = []
    for c in range(T // Q):
        rows = slice(c * Q, (c + 1) * Q)
        all_cbs.append([lax.dot_general(
            bc[rows, (SSM_GROUPS + g) * SSM_STATE:(SSM_GROUPS + g + 1) * SSM_STATE].astype(BF16),
            bc[rows, g * SSM_STATE:(g + 1) * SSM_STATE].astype(BF16), NT_DIMS, preferred_element_type=F32)
            for g in range(SSM_GROUPS)])
    out_ssm = []
    finish_prev = None
    for c in range(T // Q):
        rows = slice(c * Q, (c + 1) * Q)
        acum = jnp.dot(lower_ones, adt[rows], preferred_element_type=F32,
                       precision=lax.Precision.HIGHEST)
        acumT = jnp.dot(adtT[:, rows], upper_ones, preferred_element_type=F32,
                        precision=lax.Precision.HIGHEST)
        dtT_c = dtT[:, rows]
        wT = jnp.exp(acumT[:, Q - 1:Q] - acumT) * dtT_c
        total8 = jnp.broadcast_to(acum[Q - 1:Q, :], (8, CHUNK))
        chunk_decay = jnp.exp(jnp.dot(total8, expand, preferred_element_type=F32,
                                      precision=lax.Precision.HIGHEST))[0:1, :]
        xs_c = xs[rows]
        b_gs = [bc[rows, g * SSM_STATE:(g + 1) * SSM_STATE] for g in range(SSM_GROUPS)]
        c_gs = [bc[rows, (SSM_GROUPS + g) * SSM_STATE:(SSM_GROUPS + g + 1) * SSM_STATE].astype(BF16)
                for g in range(SSM_GROUPS)]
        s_prevs = [s_scr[g] for g in range(SSM_GROUPS)]
        cbs = all_cbs[c]
        y_offs = [jnp.dot(c_gs[g], s_prevs[g].astype(BF16), preferred_element_type=F32)
                  for g in range(SSM_GROUPS)]
        if finish_prev is not None:
            finish_prev()
        lhs_diag, lhs_state, x_bds, e_gs = [], [], [], []
        for g in range(SSM_GROUPS):
            gcols = slice(g * GROUP_W, (g + 1) * GROUP_W)
            b_gT = b_gs[g].T
            xgb = xs_c[:, gcols].astype(BF16)
            x_bds.append(jnp.concatenate([xgb * head_mask[hh] for hh in range(HEADS_PER_GROUP)], axis=0))
            ms, bws, eacs = [], [], []
            for hh in range(HEADS_PER_GROUP):
                hidx = g * HEADS_PER_GROUP + hh
                acol = jnp.broadcast_to(acum[:, hidx:hidx + 1], (Q, Q))
                arow = acumT[hidx:hidx + 1, :]
                decay = jnp.exp(jnp.where(tril, acol - arow, MASK_VALUE))
                ms.append((cbs[g] * decay * dtT_c[hidx:hidx + 1, :]).astype(BF16))
                bws.append((b_gT * wT[hidx:hidx + 1, :]).astype(BF16))
                eacs.append(jnp.exp(acol))
            e_gs.append(jnp.concatenate([jnp.where(low_half, eacs[0], eacs[1]),
                                         jnp.where(low_half, eacs[2], eacs[3])], axis=1))
            lhs_diag.append(jnp.concatenate(ms, axis=1))
            lhs_state.append(jnp.concatenate(bws, axis=1))
        y_diags = [jnp.dot(lhs_diag[g], x_bds[g], preferred_element_type=F32) for g in range(SSM_GROUPS)]
        s_incs = [jnp.dot(lhs_state[g], x_bds[g], preferred_element_type=F32) for g in range(SSM_GROUPS)]
        ys = []
        for g in range(SSM_GROUPS):
            gcols = slice(g * GROUP_W, (g + 1) * GROUP_W)
            ys.append(y_diags[g] + y_offs[g] * e_gs[g])
            s_scr[g] = chunk_decay[:, gcols] * s_prevs[g] + s_incs[g]
        def finish(ys=ys, xs_c=xs_c, rows=rows):
            y_c = jnp.concatenate(ys, axis=1) + xs_c * drep_ref[...]
            ug = y_c * _silu(z[rows])
            parts = []
            for g in range(SSM_GROUPS):
                blk = ug[:, g * GROUP_W:(g + 1) * GROUP_W]
                ms_g = jnp.mean(blk * blk, axis=-1, keepdims=True)
                parts.append(blk * lax.rsqrt(ms_g + NORM_EPS))
            yn = jnp.concatenate(parts, axis=1) * snw_ref[...]
            out_ssm.append(out_proj(yn, D_ATTN + D_CONV))

        finish_prev = finish
    finish_prev()

    out = out + jnp.concatenate(out_ssm, axis=0)
    if final_norm:
        out = _rms(out, fnw_ref[...])
    hout_ref[0] = out

    if emit_state:
        kout_ref[...] = kb
        for hd in range(ATTN_HEADS):
            vTout_ref[hd] = vTb[hd]
        tsc_out_ref[...] = ext_sc[0:TAIL, :]
        tx_out_ref[...] = ext_x[0:TAIL, :]
        s_out_ref[...] = s_scr[...]


def _const_spec(shape):
    nd = len(shape)
    return pl.BlockSpec(shape, lambda b, i, _nd=nd: (0,) * _nd, pipeline_mode=pl.Buffered(1))


def _layer_call(h, params, state, *, T, n_pad, has_prefix, emit_state, final_norm, lam_init):
    bsz, seq, _ = h.shape
    nt = seq // T
    kern = functools.partial(_layer_kernel, T=T, n_pad=n_pad, has_prefix=has_prefix,
                             emit_state=emit_state, final_norm=final_norm, lam_init=lam_init)
    ins = [h] + list(params) + list(state)
    in_specs = [pl.BlockSpec((1, T, D_MODEL), lambda b, i: (b, i, 0))]
    in_specs += [_const_spec(a.shape) for a in list(params) + list(state)]
    out_shape = [jax.ShapeDtypeStruct((bsz, seq, D_MODEL), F32)]
    out_specs = [pl.BlockSpec((1, T, D_MODEL), lambda b, i: (b, i, 0))]
    if emit_state:
        st_shapes = [((T, D_ATTN), BF16), ((ATTN_HEADS, V_ROWS, T), BF16), ((TAIL, D_CONV), F32),
                     ((TAIL, SSM_CONV_DIM), F32), ((SSM_GROUPS, SSM_STATE, GROUP_W), F32)]
        for shp, dt_ in st_shapes:
            out_shape.append(jax.ShapeDtypeStruct(shp, dt_))
            out_specs.append(pl.BlockSpec(shp, lambda b, i, _nd=len(shp): (0,) * _nd))
    scratch = [
        pltpu.VMEM((seq, D_ATTN), BF16),
        pltpu.VMEM((nt, ATTN_HEADS, V_ROWS, T), BF16),
        pltpu.VMEM((ATTN_HEADS, ATTN_V_DIM, 2 * T), BF16),
        pltpu.VMEM((ATTN_HEADS, V_ROWS, 2 * T), F32),
        pltpu.VMEM((ATTN_HEADS, T, 2 * T), F32),
        pltpu.VMEM((TAIL + T, D_CONV), F32),
        pltpu.VMEM((TAIL + T, SSM_CONV_DIM), F32),
        pltpu.VMEM((SSM_GROUPS, SSM_STATE, GROUP_W), F32),
    ]
    return pl.pallas_call(
        kern,
        grid=(bsz, nt),
        in_specs=in_specs,
        out_specs=out_specs,
        out_shape=out_shape,
        scratch_shapes=scratch,
        compiler_params=pltpu.CompilerParams(
            dimension_semantics=("arbitrary", "arbitrary"),
            vmem_limit_bytes=VMEM_LIMIT_BYTES),
        name="layer_meta" if emit_state else "layer_main",
    )(*ins)


def kernel(x, meta_tokens, norm_w, w_in, w_out, attn_lambda, attn_norm_w, sconv_w, ssm_conv_w,
           ssm_conv_b, ssm_a_log, ssm_dt_bias, ssm_d, ssm_norm_w, final_norm_w):
    pad_heads = CHUNK - SSM_HEADS
    h_meta = jnp.concatenate([jnp.zeros((N_PAD, D_MODEL), F32), meta_tokens.astype(F32)], axis=0)[None]
    h_main = x.astype(F32)
    zero_state = (
        jnp.zeros((CHUNK, D_ATTN), BF16), jnp.zeros((ATTN_HEADS, V_ROWS, CHUNK), BF16),
        jnp.zeros((TAIL, D_CONV), F32), jnp.zeros((TAIL, SSM_CONV_DIM), F32),
        jnp.zeros((SSM_GROUPS, SSM_STATE, GROUP_W), F32),
    )
    fnw = final_norm_w.astype(F32)[None, :]
    for layer in range(DEPTH):
        lam_init = 0.8 - 0.6 * math.exp(-0.3 * layer)
        w_l = w_in[layer]
        w_dt = w_l[:, OFF_DT:OFF_DT + SSM_HEADS]
        params = (
            norm_w[layer].astype(F32)[None, :],
            w_l.astype(BF16),
            jnp.pad(w_dt, ((0, 0), (0, pad_heads))).astype(BF16),
            w_dt.T.astype(BF16),
            w_out[layer].astype(BF16),
            attn_lambda[layer].astype(F32),
            attn_norm_w[layer].astype(F32)[:, None],
            sconv_w[layer].astype(F32),
            ssm_conv_w[layer].astype(F32),
            ssm_conv_b[layer].astype(F32)[None, :],
            jnp.pad(ssm_a_log[layer].astype(F32), (0, pad_heads), constant_values=-jnp.inf)[None, :],
            ssm_a_log[layer].astype(F32)[:, None],
            jnp.pad(ssm_dt_bias[layer].astype(F32), (0, pad_heads))[None, :],
            ssm_dt_bias[layer].astype(F32)[:, None],
            jnp.repeat(ssm_d[layer].astype(F32), SSM_HEAD_DIM)[None, :],
            ssm_norm_w[layer].astype(F32)[None, :],
            fnw,
        )
        meta_out = _layer_call(h_meta, params, zero_state, T=CHUNK, n_pad=N_PAD, has_prefix=False,
                               emit_state=True, final_norm=False, lam_init=lam_init)
        h_meta, meta_state = meta_out[0], tuple(meta_out[1:])
        (h_main,) = _layer_call(h_main, params, meta_state, T=MAIN_TILE, n_pad=0, has_prefix=True,
                                emit_state=False, final_norm=(layer == DEPTH - 1), lam_init=lam_init)
    return h_main.astype(x.dtype)
```

```python
import functools
import math

import jax
import jax.numpy as jnp
from jax import lax
from jax.experimental import pallas as pl
from jax.experimental.pallas import tpu as pltpu

D_MODEL = 1024
DEPTH = 4
N_META = 16
D_ATTN = 512
D_CONV = 512
D_SSM = 1024
ATTN_HEADS = 4
ATTN_QK_DIM = 64
ATTN_V_DIM = 128
SCONV_WIDTH = 3
SSM_HEAD_DIM = 64
SSM_HEADS = 16
SSM_GROUPS = 4
SSM_STATE = 128
SSM_CONV_WIDTH = 4
SSM_CONV_DIM = 2048
CHUNK = 128
N_PAD = CHUNK - N_META
NORM_EPS = 1e-5
MASK_VALUE = -1e30
LOG2_E = 1.4426950408889634
V_ROWS = ATTN_V_DIM + 16
GROUP_W = D_SSM // SSM_GROUPS
HEADS_PER_GROUP = SSM_HEADS // SSM_GROUPS

OFF_QKV = 0
OFF_GATE_A = 3 * D_ATTN
OFF_SC = 4 * D_ATTN
OFF_Z = OFF_SC + 4 * D_CONV
OFF_XBC = OFF_Z + D_SSM
OFF_DT = OFF_XBC + SSM_CONV_DIM

TAIL = 8
MAIN_TILE = 256
VMEM_LIMIT_BYTES = 56 * 1024 * 1024

F32 = jnp.float32
BF16 = jnp.bfloat16
NT_DIMS = (((1,), (1,)), ((), ()))


def _silu(x):
    half = 0.5 * x
    return half * jnp.tanh(half) + half


def _softplus(x):
    return jnp.maximum(x, 0.0) + jnp.log1p(jnp.exp(-jnp.abs(x)))


def _rms(x, w):
    ms = jnp.mean(x * x, axis=-1, keepdims=True)
    return x * lax.rsqrt(ms + NORM_EPS) * w


def _attn_scores(q_scr, k_blk_fn):
    return [jnp.dot(k_blk_fn(hd), q_scr[hd], preferred_element_type=F32)
            for hd in range(ATTN_HEADS)]


def _attn_softmax(s2s, maxes, mask, first, T):
    new, ps, alphas = [], [], []
    for hd in range(ATTN_HEADS):
        for mp in range(2):
            s = s2s[hd][:, mp * T:(mp + 1) * T]
            if mask is not None:
                s = jnp.where(mask, s, MASK_VALUE)
            m_new = jnp.max(s, axis=0, keepdims=True)
            if not first:
                m_old = maxes[2 * hd + mp]
                m_new = jnp.maximum(m_old, m_new)
                alphas.append(jnp.exp2(m_old - m_new))
            ps.append(jnp.exp2(s - m_new).astype(BF16))
            new.append(m_new)
    return tuple(new), ps, alphas


def _attn_accumulate(acc_scr, ps, alphas, vT_blk_fn, first):
    for hd in range(ATTN_HEADS):
        pv = jnp.dot(vT_blk_fn(hd), jnp.concatenate(ps[2 * hd:2 * hd + 2], axis=1),
                     preferred_element_type=F32)
        if first:
            acc_scr[hd] = pv
        else:
            acc_scr[hd] = jnp.concatenate(alphas[2 * hd:2 * hd + 2], axis=1) * acc_scr[hd] + pv


def _layer_kernel(h_ref, nw_ref, win_ref, wdt_ref, wdtT_ref, wout_ref, lam_ref, anw_ref,
                  scw_ref, xw_ref, xb_ref, alog_row_ref, alog_col_ref, dtb_row_ref, dtb_col_ref,
                  drep_ref, snw_ref, fnw_ref, kpre_ref, vpreT_ref, tsc_ref, tx_ref, s0_ref,
                  *rest, T, n_pad, has_prefix, emit_state, final_norm, lam_init):
    if emit_state:
        (hout_ref, kout_ref, vTout_ref, tsc_out_ref, tx_out_ref, s_out_ref,
         k_scr, vT_scr, q_scr, acc_scr, s2_scr, ext_sc, ext_x, s_scr) = rest
    else:
        hout_ref, k_scr, vT_scr, q_scr, acc_scr, s2_scr, ext_sc, ext_x, s_scr = rest

    i = pl.program_id(1)
    Q = CHUNK

    @pl.when(i == 0)
    def _init():
        ext_sc[0:TAIL, :] = tsc_ref[...]
        ext_x[0:TAIL, :] = tx_ref[...]
        s_scr[...] = s0_ref[...]

    if n_pad:
        row_ok = lax.broadcasted_iota(jnp.int32, (T, 1), 0) >= n_pad
        col_ok = lax.broadcasted_iota(jnp.int32, (1, T), 1) >= n_pad

    h = h_ref[0]
    u = _rms(h, nw_ref[...])
    if n_pad:
        u = jnp.where(row_ok, u, 0.0)
    ub = u.astype(BF16)

    def proj(off, width):
        return jnp.dot(ub, win_ref[:, off:off + width], preferred_element_type=F32)

    def out_proj(y, off):
        return jnp.dot(y.astype(BF16), wout_ref[off:off + y.shape[1], :], preferred_element_type=F32)

    def causal_conv(raw, ext_ref, cols, w_ref, width, bias):
        ext = jnp.concatenate([ext_ref[0:TAIL, cols], raw], axis=0)
        acc = raw * w_ref[width - 1:width, cols]
        if bias is not None:
            acc = acc + bias
        for j in range(width - 1):
            shift = width - 1 - j
            acc = acc + pltpu.roll(ext, shift, 0)[TAIL:TAIL + T] * w_ref[j:j + 1, cols]
        ext_ref[0:TAIL, cols] = raw[T - TAIL:T]
        return acc

    def head_cols(hd):
        return slice(hd * ATTN_V_DIM, (hd + 1) * ATTN_V_DIM)

    qkv = proj(OFF_QKV, 3 * D_ATTN)
    sc = proj(OFF_SC, 4 * D_CONV)
    kb = qkv[:, D_ATTN:2 * D_ATTN].astype(BF16)
    vT = qkv[:, 2 * D_ATTN:3 * D_ATTN].T
    ones_rows = (lax.broadcasted_iota(jnp.int32, (V_ROWS - ATTN_V_DIM, T), 0) == 0).astype(BF16)
    vTb = [jnp.concatenate([vT[hd * ATTN_V_DIM:(hd + 1) * ATTN_V_DIM].astype(BF16), ones_rows], axis=0)
           for hd in range(ATTN_HEADS)]
    k_scr[pl.ds(pl.multiple_of(i * T, CHUNK), T), :] = kb
    for hd in range(ATTN_HEADS):
        vT_scr[i, hd] = vTb[hd]

    q = qkv[:, 0:D_ATTN] * (ATTN_QK_DIM ** -0.5 * LOG2_E)
    lane = lax.broadcasted_iota(jnp.int32, (T, ATTN_V_DIM), 1)
    for hd in range(ATTN_HEADS):
        qh = q[:, head_cols(hd)]
        q_scr[hd, :, 0:T] = jnp.where(lane < ATTN_QK_DIM, qh, 0.0).T.astype(BF16)
        q_scr[hd, :, T:2 * T] = jnp.where(lane >= ATTN_QK_DIM, qh, 0.0).T.astype(BF16)

    cch = sc[:, D_CONV:2 * D_CONV] * sc[:, 2 * D_CONV:3 * D_CONV]
    conv = causal_conv(cch, ext_sc, slice(0, D_CONV), scw_ref, SCONV_WIDTH, None)
    y_conv = sc[:, 0:D_CONV] * conv * _silu(sc[:, 3 * D_CONV:4 * D_CONV])

    def tile_keys(j):
        off = pl.multiple_of(j * T, CHUNK)
        return lambda hd: k_scr[pl.ds(off, T), head_cols(hd)]

    def stash_scores(s2s):
        for hd in range(ATTN_HEADS):
            s2_scr[hd] = s2s[hd]

    maxes = None
    if has_prefix:
        prefix_mask = lax.broadcasted_iota(jnp.int32, (CHUNK, T), 0) >= N_PAD
        s2p = _attn_scores(q_scr, lambda hd: kpre_ref[:, head_cols(hd)])
        maxes, ps, alphas = _attn_softmax(s2p, None, prefix_mask, True, T)
        s2n = _attn_scores(q_scr, tile_keys(0))
        _attn_accumulate(acc_scr, ps, alphas, lambda hd: vpreT_ref[hd], True)
        stash_scores(s2n)

        def body(j, mx):
            new, ps, alphas = _attn_softmax([s2_scr[hd] for hd in range(ATTN_HEADS)], mx, None, False, T)
            s2n = _attn_scores(q_scr, tile_keys(j + 1))
            _attn_accumulate(acc_scr, ps, alphas, lambda hd: vT_scr[j, hd], False)
            stash_scores(s2n)
            return new

        maxes = lax.fori_loop(0, i, body, maxes)
        s2d = [s2_scr[hd] for hd in range(ATTN_HEADS)]
    else:
        s2d = _attn_scores(q_scr, lambda hd: kb[:, head_cols(hd)])

    key_i = lax.broadcasted_iota(jnp.int32, (T, T), 0)
    qry_i = lax.broadcasted_iota(jnp.int32, (T, T), 1)
    diag_mask = key_i <= qry_i
    if n_pad:
        diag_mask = jnp.logical_and(diag_mask, key_i >= n_pad)
    _, ps, alphas = _attn_softmax(s2d, maxes, diag_mask, not has_prefix, T)
    xs_raw = proj(OFF_XBC, D_SSM)
    _attn_accumulate(acc_scr, ps, alphas, lambda hd: vTb[hd], not has_prefix)
    bc_raw = proj(OFF_XBC + D_SSM, SSM_CONV_DIM - D_SSM)
    xs = _silu(causal_conv(xs_raw, ext_x, slice(0, D_SSM), xw_ref, SSM_CONV_WIDTH, xb_ref[:, 0:D_SSM]))
    gate_a = proj(OFF_GATE_A, D_ATTN)
    z = proj(OFF_Z, D_SSM)
    bc = _silu(causal_conv(bc_raw, ext_x, slice(D_SSM, SSM_CONV_DIM), xw_ref, SSM_CONV_WIDTH,
                           xb_ref[:, D_SSM:SSM_CONV_DIM]))
    if n_pad:
        xs = jnp.where(row_ok, xs, 0.0)
        bc = jnp.where(row_ok, bc, 0.0)

    lp = lam_ref[...]
    lam = (jnp.exp(jnp.sum(lp[0:1] * lp[1:2], axis=-1, keepdims=True))
           - jnp.exp(jnp.sum(lp[2:3] * lp[3:4], axis=-1, keepdims=True)) + lam_init)
    y_attn = []
    for hd in range(ATTN_HEADS):
        dv = ATTN_V_DIM
        inv_l0 = 1.0 / acc_scr[hd, dv:dv + 1, 0:T]
        inv_l1 = 1.0 / acc_scr[hd, dv:dv + 1, T:2 * T]
        oT = acc_scr[hd, 0:dv, 0:T] * inv_l0 - lam * (acc_scr[hd, 0:dv, T:2 * T] * inv_l1)
        ms = jnp.mean(oT * oT, axis=0, keepdims=True)
        oT = oT * lax.rsqrt(ms + NORM_EPS) * anw_ref[...] * (1.0 - lam_init)
        y_attn.append(oT.T * _silu(gate_a[:, head_cols(hd)]))
    out = h_ref[0] + out_proj(y_conv, D_ATTN)
    out = out + out_proj(jnp.concatenate(y_attn, axis=1), 0)

    dt = _softplus(jnp.dot(ub, wdt_ref[...], preferred_element_type=F32) + dtb_row_ref[...])
    dtT = _softplus(lax.dot_general(wdtT_ref[...], ub, NT_DIMS, preferred_element_type=F32)
                    + dtb_col_ref[...])
    if n_pad:
        dt = jnp.where(row_ok, dt, 0.0)
        dtT = jnp.where(col_ok, dtT, 0.0)
    adt = dt * (-jnp.exp(alog_row_ref[...]))
    adtT = dtT * (-jnp.exp(alog_col_ref[...]))

    qr = lax.broadcasted_iota(jnp.int32, (Q, Q), 0)
    qc = lax.broadcasted_iota(jnp.int32, (Q, Q), 1)
    tril = qr >= qc
    lower_ones = tril.astype(F32)
    upper_ones = (qr <= qc).astype(F32)
    expand = (lax.broadcasted_iota(jnp.int32, (CHUNK, D_SSM), 0)
              == (lax.broadcasted_iota(jnp.int32, (CHUNK, D_SSM), 1) // SSM_HEAD_DIM)).astype(F32)
    colblk = lax.broadcasted_iota(jnp.int32, (Q, GROUP_W), 1) // SSM_HEAD_DIM
    head_mask = [jnp.where(colblk == hh, 1.0, 0.0).astype(BF16) for hh in range(HEADS_PER_GROUP)]
    low_half = lax.broadcasted_iota(jnp.int32, (Q, CHUNK), 1) < SSM_HEAD_DIM

    all_cbs = []
    for c in range(T // Q):
        rows = slice(c * Q, (c + 1) * Q)
        all_cbs.append([lax.dot_general(
            bc[rows, (SSM_GROUPS + g) * SSM_STATE:(SSM_GROUPS + g + 1) * SSM_STATE].astype(BF16),
            bc[rows, g * SSM_STATE:(g + 1) * SSM_STATE].astype(BF16), NT_DIMS, preferred_element_type=F32)
            for g in range(SSM_GROUPS)])
    out_ssm = []
    finish_prev = None
    for c in range(T // Q):
        rows = slice(c * Q, (c + 1) * Q)
        acum = jnp.dot(lower_ones, adt[rows], preferred_element_type=F32,
                       precision=lax.Precision.HIGHEST)
        acumT = jnp.dot(adtT[:, rows], upper_ones, preferred_element_type=F32,
                        precision=lax.Precision.HIGHEST)
        dtT_c = dtT[:, rows]
        wT = jnp.exp(acumT[:, Q - 1:Q] - acumT) * dtT_c
        total8 = jnp.broadcast_to(acum[Q - 1:Q, :], (8, CHUNK))
        chunk_decay = jnp.exp(jnp.dot(total8, expand, preferred_element_type=F32,
                                      precision=lax.Precision.HIGHEST))[0:1, :]
        xs_c = xs[rows]
        b_gs = [bc[rows, g * SSM_STATE:(g + 1) * SSM_STATE] for g in range(SSM_GROUPS)]
        c_gs = [bc[rows, (SSM_GROUPS + g) * SSM_STATE:(SSM_GROUPS + g + 1) * SSM_STATE].astype(BF16)
                for g in range(SSM_GROUPS)]
        s_prevs = [s_scr[g] for g in range(SSM_GROUPS)]
        cbs = all_cbs[c]
        y_offs = [jnp.dot(c_gs[g], s_prevs[g].astype(BF16), preferred_element_type=F32)
                  for g in range(SSM_GROUPS)]
        if finish_prev is not None:
            finish_prev()
        lhs_diag, lhs_state, x_bds, e_gs = [], [], [], []
        for g in range(SSM_GROUPS):
            gcols = slice(g * GROUP_W, (g + 1) * GROUP_W)
            b_gT = b_gs[g].T
            xgb = xs_c[:, gcols].astype(BF16)
            x_bds.append(jnp.concatenate([xgb * head_mask[hh] for hh in range(HEADS_PER_GROUP)], axis=0))
            ms, bws, eacs = [], [], []
            for hh in range(HEADS_PER_GROUP):
                hidx = g * HEADS_PER_GROUP + hh
                acol = jnp.broadcast_to(acum[:, hidx:hidx + 1], (Q, Q))
                arow = acumT[hidx:hidx + 1, :]
                decay = jnp.exp(jnp.where(tril, acol - arow, MASK_VALUE))
                ms.append((cbs[g] * decay * dtT_c[hidx:hidx + 1, :]).astype(BF16))
                bws.append((b_gT * wT[hidx:hidx + 1, :]).astype(BF16))
                eacs.append(jnp.exp(acol))
            e_gs.append(jnp.concatenate([jnp.where(low_half, eacs[0], eacs[1]),
                                         jnp.where(low_half, eacs[2], eacs[3])], axis=1))
            lhs_diag.append(jnp.concatenate(ms, axis=1))
            lhs_state.append(jnp.concatenate(bws, axis=1))
        y_diags = [jnp.dot(lhs_diag[g], x_bds[g], preferred_element_type=F32) for g in range(SSM_GROUPS)]
        s_incs = [jnp.dot(lhs_state[g], x_bds[g], preferred_element_type=F32) for g in range(SSM_GROUPS)]
        ys = []
        for g in range(SSM_GROUPS):
            gcols = slice(g * GROUP_W, (g + 1) * GROUP_W)
            ys.append(y_diags[g] + y_offs[g] * e_gs[g])
            s_scr[g] = chunk_decay[:, gcols] * s_prevs[g] + s_incs[g]
        def finish(ys=ys, xs_c=xs_c, rows=rows):
            y_c = jnp.concatenate(ys, axis=1) + xs_c * drep_ref[...]
            ug = y_c * _silu(z[rows])
            parts = []
            for g in range(SSM_GROUPS):
                blk = ug[:, g * GROUP_W:(g + 1) * GROUP_W]
                ms_g = jnp.mean(blk * blk, axis=-1, keepdims=True)
                parts.append(blk * lax.rsqrt(ms_g + NORM_EPS))
            yn = jnp.concatenate(parts, axis=1) * snw_ref[...]
            out_ssm.append(out_proj(yn, D_ATTN + D_CONV))

        finish_prev = finish
    finish_prev()

    out = out + jnp.concatenate(out_ssm, axis=0)
    if final_norm:
        out = _rms(out, fnw_ref[...])
    hout_ref[0] = out

    if emit_state:
        kout_ref[...] = kb
        for hd in range(ATTN_HEADS):
            vTout_ref[hd] = vTb[hd]
        tsc_out_ref[...] = ext_sc[0:TAIL, :]
        tx_out_ref[...] = ext_x[0:TAIL, :]
        s_out_ref[...] = s_scr[...]


def _const_spec(shape):
    nd = len(shape)
    return pl.BlockSpec(shape, lambda b, i, _nd=nd: (0,) * _nd, pipeline_mode=pl.Buffered(1))


def _layer_call(h, params, state, *, T, n_pad, has_prefix, emit_state, final_norm, lam_init):
    bsz, seq, _ = h.shape
    nt = seq // T
    kern = functools.partial(_layer_kernel, T=T, n_pad=n_pad, has_prefix=has_prefix,
                             emit_state=emit_state, final_norm=final_norm, lam_init=lam_init)
    ins = [h] + list(params) + list(state)
    in_specs = [pl.BlockSpec((1, T, D_MODEL), lambda b, i: (b, i, 0))]
    in_specs += [_const_spec(a.shape) for a in list(params) + list(state)]
    out_shape = [jax.ShapeDtypeStruct((bsz, seq, D_MODEL), F32)]
    out_specs = [pl.BlockSpec((1, T, D_MODEL), lambda b, i: (b, i, 0))]
    if emit_state:
        st_shapes = [((T, D_ATTN), BF16), ((ATTN_HEADS, V_ROWS, T), BF16), ((TAIL, D_CONV), F32),
                     ((TAIL, SSM_CONV_DIM), F32), ((SSM_GROUPS, SSM_STATE, GROUP_W), F32)]
        for shp, dt_ in st_shapes:
            out_shape.append(jax.ShapeDtypeStruct(shp, dt_))
            out_specs.append(pl.BlockSpec(shp, lambda b, i, _nd=len(shp): (0,) * _nd))
    scratch = [
        pltpu.VMEM((seq, D_ATTN), BF16),
        pltpu.VMEM((nt, ATTN_HEADS, V_ROWS, T), BF16),
        pltpu.VMEM((ATTN_HEADS, ATTN_V_DIM, 2 * T), BF16),
        pltpu.VMEM((ATTN_HEADS, V_ROWS, 2 * T), F32),
        pltpu.VMEM((ATTN_HEADS, T, 2 * T), F32),
        pltpu.VMEM((TAIL + T, D_CONV), F32),
        pltpu.VMEM((TAIL + T, SSM_CONV_DIM), F32),
        pltpu.VMEM((SSM_GROUPS, SSM_STATE, GROUP_W), F32),
    ]
    return pl.pallas_call(
        kern,
        grid=(bsz, nt),
        in_specs=in_specs,
        out_specs=out_specs,
        out_shape=out_shape,
        scratch_shapes=scratch,
        compiler_params=pltpu.CompilerParams(
            dimension_semantics=("arbitrary", "arbitrary"),
            vmem_limit_bytes=VMEM_LIMIT_BYTES),
        name="layer_meta" if emit_state else "layer_main",
    )(*ins)


def kernel(x, meta_tokens, norm_w, w_in, w_out, attn_lambda, attn_norm_w, sconv_w, ssm_conv_w,
           ssm_conv_b, ssm_a_log, ssm_dt_bias, ssm_d, ssm_norm_w, final_norm_w):
    pad_heads = CHUNK - SSM_HEADS
    h_meta = jnp.concatenate([jnp.zeros((N_PAD, D_MODEL), F32), meta_tokens.astype(F32)], axis=0)[None]
    h_main = x.astype(F32)
    zero_state = (
        jnp.zeros((CHUNK, D_ATTN), BF16), jnp.zeros((ATTN_HEADS, V_ROWS, CHUNK), BF16),
        jnp.zeros((TAIL, D_CONV), F32), jnp.zeros((TAIL, SSM_CONV_DIM), F32),
        jnp.zeros((SSM_GROUPS, SSM_STATE, GROUP_W), F32),
    )
    fnw = final_norm_w.astype(F32)[None, :]
    for layer in range(DEPTH):
        lam_init = 0.8 - 0.6 * math.exp(-0.3 * layer)
        w_l = w_in[layer]
        w_dt = w_l[:, OFF_DT:OFF_DT + SSM_HEADS]
        params = (
            norm_w[layer].astype(F32)[None, :],
            w_l.astype(BF16),
            jnp.pad(w_dt, ((0, 0), (0, pad_heads))).astype(BF16),
            w_dt.T.astype(BF16),
            w_out[layer].astype(BF16),
            attn_lambda[layer].astype(F32),
            attn_norm_w[layer].astype(F32)[:, None],
            sconv_w[layer].astype(F32),
            ssm_conv_w[layer].astype(F32),
            ssm_conv_b[layer].astype(F32)[None, :],
            jnp.pad(ssm_a_log[layer].astype(F32), (0, pad_heads), constant_values=-jnp.inf)[None, :],
            ssm_a_log[layer].astype(F32)[:, None],
            jnp.pad(ssm_dt_bias[layer].astype(F32), (0, pad_heads))[None, :],
            ssm_dt_bias[layer].astype(F32)[:, None],
            jnp.repeat(ssm_d[layer].astype(F32), SSM_HEAD_DIM)[None, :],
            ssm_norm_w[layer].astype(F32)[None, :],
            fnw,
        )
        meta_out = _layer_call(h_meta, params, zero_state, T=CHUNK, n_pad=N_PAD, has_prefix=False,
                               emit_state=True, final_norm=False, lam_init=lam_init)
        h_meta, meta_state = meta_out[0], tuple(meta_out[1:])
        (h_main,) = _layer_call(h_main, params, meta_state, T=MAIN_TILE, n_pad=0, has_prefix=True,
                                emit_state=False, final_norm=(layer == DEPTH - 1), lam_init=lam_init)
    return h_main.astype(x.dtype)
```
